```python
import math
import jax, jax.numpy as jnp
from jax import lax
import numpy as np

D_MODEL = 2048
BATCH = 1
SEQ = 8192
DEPTH = 4

HEAD_DIM = 128
N_HEADS_TOTAL = D_MODEL // HEAD_DIM
N_HEADS_C = N_HEADS_TOTAL // 4
N_HEADS_A = (N_HEADS_TOTAL - N_HEADS_C) // 2
N_HEADS_B = N_HEADS_TOTAL - N_HEADS_C - N_HEADS_A
DIFF_QK_DIM = HEAD_DIM // 2
WIDTH_A = N_HEADS_A * HEAD_DIM
WIDTH_B = N_HEADS_B * HEAD_DIM
WIDTH_C = N_HEADS_C * HEAD_DIM
MIX_WIDTH = WIDTH_A + WIDTH_B + WIDTH_C
IN_COLS = 3 * WIDTH_A + 3 * WIDTH_B + 2 * WIDTH_C
DILATION_PATTERNS = ((128, 1), (512, 4), (2048, 16))
BLK = 128
CHUNK = 128
ROPE_THETA = 500000.0
ROPE_FRACTION = 4
FFN_DIM = 5632
CONV_WIDTH = 3
EPS = 1e-6

kernel_name = "hybrid_dilated_diff_sgu_convffn"


def rmsnorm(x, g):
    xf = x.astype(jnp.float32)
    y = xf * lax.rsqrt(jnp.mean(xf * xf, axis=-1, keepdims=True) + EPS)
    return (y * g.astype(jnp.float32)).astype(x.dtype)


def layernorm(x, g, b):
    xf = x.astype(jnp.float32)
    mu = jnp.mean(xf, axis=-1, keepdims=True)
    var = jnp.mean(jnp.square(xf - mu), axis=-1, keepdims=True)
    y = (xf - mu) * lax.rsqrt(var + EPS)
    return (y * g.astype(jnp.float32) + b.astype(jnp.float32)).astype(x.dtype)


def rope_tables(seq, head_dim):
    rot_dim = head_dim // ROPE_FRACTION
    inv = 1.0 / (ROPE_THETA ** (jnp.arange(0, rot_dim, 2, dtype=jnp.float32) / rot_dim))
    ang = jnp.arange(seq, dtype=jnp.float32)[:, None] * inv[None, :]
    return jnp.cos(ang), jnp.sin(ang)


def rope_partial(x, cos, sin):
    rd = cos.shape[-1] * 2
    xr = x[..., :rd].astype(jnp.float32)
    x1, x2 = xr[..., : rd // 2], xr[..., rd // 2:]
    c = cos[None, :, None, :]
    s = sin[None, :, None, :]
    rot = jnp.concatenate([x1 * c - x2 * s, x2 * c + x1 * s], axis=-1)
    return jnp.concatenate([rot.astype(x.dtype), x[..., rd:]], axis=-1)


def banded_window_attn(q, k, v, window):
    N, L, H, dh = q.shape
    nb = L // BLK
    scale = dh ** -0.5
    qb = q.reshape(N, nb, BLK, H, dh)
    kb = k.reshape(N, nb, BLK, H, dh)
    vb = v.reshape(N, nb, BLK, H, dh)
    pad = ((0, 0), (1, 0), (0, 0), (0, 0), (0, 0))
    kcat = jnp.concatenate([jnp.pad(kb, pad)[:, :-1], kb], axis=2)
    vcat = jnp.concatenate([jnp.pad(vb, pad)[:, :-1], vb], axis=2)
    s = jnp.einsum('nbqhd,nbkhd->nbhqk', qb, kcat, preferred_element_type=jnp.float32) * scale
    qi = jnp.arange(BLK)[:, None] + BLK
    kj = jnp.arange(2 * BLK)[None, :]
    dist = qi - kj
    band = (dist >= 0) & (dist <= window)
    has_prev = (jnp.arange(nb) > 0)[:, None, None] | (kj >= BLK)[None]
    mask = band[None] & has_prev
    s = jnp.where(mask[None, :, None], s, -jnp.inf)
    m = jnp.max(s, axis=-1, keepdims=True)
    p = jnp.exp(s - m)
    l = jnp.sum(p, axis=-1)
    o = jnp.einsum('nbhqk,nbkhd->nbqhd', p.astype(v.dtype), vcat, preferred_element_type=jnp.float32)
    o = o / jnp.transpose(l, (0, 1, 3, 2))[..., None]
    lse = jnp.transpose(m[..., 0] + jnp.log(l), (0, 1, 3, 2))
    return o.reshape(N, L, H, dh), lse.reshape(N, L, H)


def dilated_window_attn(q, k, v, window, dilation):
    B, S, H, dh = q.shape
    span = dilation * BLK
    Sp = -(-S // span) * span
    Ls = Sp // dilation

    def to_sub(t):
        t = jnp.pad(t, ((0, 0), (0, Sp - S), (0, 0), (0, 0)))
        return t.reshape(B, Ls, dilation, H, dh).transpose(0, 2, 1, 3, 4).reshape(B * dilation, Ls, H, dh)

    o, lse = banded_window_attn(to_sub(q), to_sub(k), to_sub(v), window // dilation)
    o = o.reshape(B, dilation, Ls, H, dh).transpose(0, 2, 1, 3, 4).reshape(B, Sp, H, dh)[:, :S]
    lse = lse.reshape(B, dilation, Ls, H).transpose(0, 2, 1, 3).reshape(B, Sp, H)[:, :S]
    return o, lse


def dilated_mixture(q, k, v):
    outs, lses = [], []
    for window, dilation in DILATION_PATTERNS:
        o, lse = dilated_window_attn(q, k, v, window, dilation)
        outs.append(o)
        lses.append(lse)
    w = jax.nn.softmax(jnp.stack(lses, axis=0), axis=0)
    o = jnp.sum(w[..., None] * jnp.stack(outs, axis=0), axis=0)
    return o.astype(q.dtype)


def diff_attention(q1, q2, k1, k2, v, lam):
    B, S, H, dq = q1.shape
    dv = v.shape[-1]
    nq = S // BLK
    scale = dq ** -0.5
    kpos = jnp.arange(S)

    def split(t):
        return t.reshape(B, nq, BLK, H, t.shape[-1]).transpose(1, 0, 2, 3, 4)

    def step(args):
        qa, qb, bi = args
        qpos = bi * BLK + jnp.arange(BLK)
        mask = qpos[:, None] >= kpos[None, :]

        def probs(qq, kk):
            s = jnp.einsum('bqhd,bkhd->bhqk', qq, kk, preferred_element_type=jnp.float32) * scale
            return jax.nn.softmax(jnp.where(mask, s, -jnp.inf), axis=-1)

        p = probs(qa, k1) - lam * probs(qb, k2)
        return jnp.einsum('bhqk,bkhd->bqhd', p.astype(v.dtype), v, preferred_element_type=jnp.float32).astype(v.dtype)

    out = lax.map(step, (split(q1), split(q2), jnp.arange(nq)))
    return out.transpose(1, 0, 2, 3, 4).reshape(B, S, H, dv)


def spatial_gating(u, v, ln_g, ln_b, w_s, b_s):
    B, S, _ = u.shape
    nc = S // CHUNK
    vn = layernorm(v, ln_g, ln_b).reshape(B, nc, CHUNK, N_HEADS_C, HEAD_DIM)
    tri = jnp.tril(jnp.ones((CHUNK, CHUNK), dtype=bool))
    wm = jnp.where(tri[None], w_s, jnp.zeros((), w_s.dtype))
    y = jnp.einsum('gij,bcjge->bcige', wm, vn) + jnp.transpose(b_s)[None, None, :, :, None]
    return u * y.reshape(B, S, WIDTH_C)


def conv_gated_mlp(h, w_up, conv_w, conv_b, w_down):
    up = h @ w_up
    hp = jnp.pad(up, ((0, 0), (CONV_WIDTH - 1, 0), (0, 0)))
    S = up.shape[1]
    c = conv_b + sum(conv_w[i] * hp[:, i:i + S] for i in range(CONV_WIDTH))
    gate, val = jnp.split(c, 2, axis=-1)
    return (jax.nn.silu(gate) * val) @ w_down


def setup_inputs(seed: int = 0) -> dict:
    key = jax.random.key(seed)
    ks = jax.random.split(key, 20)
    f = jnp.float32
    nrm = lambda k, shape, s: jax.random.normal(k, shape, f) * s
    return {
        "x": nrm(ks[0], (BATCH, SEQ, D_MODEL), 1.0),
        "norm_mix": 1.0 + nrm(ks[1], (DEPTH, D_MODEL), 0.02),
        "w_in": nrm(ks[2], (DEPTH, D_MODEL, IN_COLS), D_MODEL ** -0.5),
        "lambda_q1": nrm(ks[3], (DEPTH, DIFF_QK_DIM), 0.1),
        "lambda_k1": nrm(ks[4], (DEPTH, DIFF_QK_DIM), 0.1),
        "lambda_q2": nrm(ks[5], (DEPTH, DIFF_QK_DIM), 0.1),
        "lambda_k2": nrm(ks[6], (DEPTH, DIFF_QK_DIM), 0.1),
        "diff_subln": 1.0 + nrm(ks[7], (DEPTH, HEAD_DIM), 0.02),
        "sgu_ln_g": 1.0 + nrm(ks[8], (DEPTH, WIDTH_C), 0.02),
        "sgu_ln_b": nrm(ks[9], (DEPTH, WIDTH_C), 0.02),
        "sgu_w": nrm(ks[10], (DEPTH, N_HEADS_C, CHUNK, CHUNK), CHUNK ** -0.5),
        "sgu_b": 1.0 + nrm(ks[11], (DEPTH, N_HEADS_C, CHUNK), 0.02),
        "w_out": nrm(ks[12], (DEPTH, MIX_WIDTH, D_MODEL), MIX_WIDTH ** -0.5),
        "norm_ffn": 1.0 + nrm(ks[13], (DEPTH, D_MODEL), 0.02),
        "w_up": nrm(ks[14], (DEPTH, D_MODEL, 2 * FFN_DIM), D_MODEL ** -0.5),
        "conv_w": nrm(ks[15], (DEPTH, CONV_WIDTH, 2 * FFN_DIM), CONV_WIDTH ** -0.5),
        "conv_b": nrm(ks[16], (DEPTH, 2 * FFN_DIM), 0.01),
        "w_down": nrm(ks[17], (DEPTH, FFN_DIM, D_MODEL), FFN_DIM ** -0.5),
        "norm_final": 1.0 + nrm(ks[18], (D_MODEL,), 0.02),
    }


def reference(x, norm_mix, w_in, lambda_q1, lambda_k1, lambda_q2, lambda_k2, diff_subln,
              sgu_ln_g, sgu_ln_b, sgu_w, sgu_b, w_out, norm_ffn, w_up, conv_w, conv_b, w_down,
              norm_final):
    B, S, _ = x.shape
    cos_a, sin_a = rope_tables(S, HEAD_DIM)
    cos_b, sin_b = rope_tables(S, DIFF_QK_DIM)
    splits = np.cumsum([WIDTH_A, WIDTH_A, WIDTH_A, WIDTH_B, WIDTH_B, WIDTH_B, WIDTH_C])
    for l in range(DEPTH):
        h = rmsnorm(x, norm_mix[l])
        proj = h @ w_in[l]
        qa, ka, va, qb, kb, vb, u, v = jnp.split(proj, splits, axis=-1)
        qa = rope_partial(qa.reshape(B, S, N_HEADS_A, HEAD_DIM), cos_a, sin_a)
        ka = rope_partial(ka.reshape(B, S, N_HEADS_A, HEAD_DIM), cos_a, sin_a)
        va = va.reshape(B, S, N_HEADS_A, HEAD_DIM)
        out_a = dilated_mixture(qa, ka, va).reshape(B, S, WIDTH_A)
        qb = qb.reshape(B, S, N_HEADS_B, 2, DIFF_QK_DIM)
        kb = kb.reshape(B, S, N_HEADS_B, 2, DIFF_QK_DIM)
        q1 = rope_partial(qb[..., 0, :], cos_b, sin_b)
        q2 = rope_partial(qb[..., 1, :], cos_b, sin_b)
        k1 = rope_partial(kb[..., 0, :], cos_b, sin_b)
        k2 = rope_partial(kb[..., 1, :], cos_b, sin_b)
        lambda_init = 0.8 - 0.6 * math.exp(-0.3 * l)
        lam = (jnp.exp(jnp.sum(lambda_q1[l].astype(jnp.float32) * lambda_k1[l].astype(jnp.float32)))
               - jnp.exp(jnp.sum(lambda_q2[l].astype(jnp.float32) * lambda_k2[l].astype(jnp.float32)))
               + lambda_init)
        ob = diff_attention(q1, q2, k1, k2, vb.reshape(B, S, N_HEADS_B, HEAD_DIM), lam)
        out_b = (rmsnorm(ob, diff_subln[l]) * (1.0 - lambda_init)).reshape(B, S, WIDTH_B)
        out_c = spatial_gating(jax.nn.gelu(u, approximate=False), jax.nn.gelu(v, approximate=False),
                               sgu_ln_g[l], sgu_ln_b[l], sgu_w[l], sgu_b[l])
        mix = jnp.concatenate([out_a, out_b.astype(x.dtype), out_c], axis=-1)
        x = x + mix @ w_out[l]
        x = x + conv_gated_mlp(rmsnorm(x, norm_ffn[l]), w_up[l], conv_w[l], conv_b[l], w_down[l])
    return rmsnorm(x, norm_final)
```

```python
import functools
import math

import jax
import jax.numpy as jnp
from jax import lax
from jax.experimental import pallas as pl
from jax.experimental.pallas import tpu as pltpu

F32 = jnp.float32
BF16 = jnp.bfloat16

D_MODEL = 2048
HEAD_DIM = 128
N_HEADS_A = 6
N_HEADS_B = 6
N_HEADS_C = 4
DIFF_QK_DIM = 64
WIDTH_A = N_HEADS_A * HEAD_DIM
WIDTH_B = N_HEADS_B * HEAD_DIM
WIDTH_C = N_HEADS_C * HEAD_DIM
ATTN_COLS = 3 * WIDTH_A + 3 * WIDTH_B
DILATIONS = (1, 4, 16)
BLK = 128
SUPER = BLK * DILATIONS[-1]
CHUNK = 128
ROPE_THETA = 500000.0
ROPE_FRACTION = 4
FFN_DIM = 5632
CONV_WIDTH = 3
EPS = 1e-6
LOG2E = math.log2(math.e)
MASKED = -1e30

LANES = 128
SUBLANES = 8
VMEM_LIMIT = 56 * 1024 * 1024


def _params(semantics, vmem=VMEM_LIMIT):
    return pltpu.CompilerParams(dimension_semantics=semantics, vmem_limit_bytes=vmem)


def _rms_scale(x, gain):
    return x * lax.rsqrt(jnp.mean(x * x, axis=-1, keepdims=True) + EPS) * gain


def _dot_nt(a, b):
    return lax.dot_general(a, b, (((1,), (1,)), ((), ())), preferred_element_type=F32)


def _dot(a, b):
    return jnp.dot(a, b, preferred_element_type=F32)


def _in_proj_kernel(x_ref, gain_ref, w_ref, tab_ref, oa_ref, ob_ref, h_ref):
    j = pl.program_id(1)

    @pl.when(j == 0)
    def _():
        h_ref[...] = _rms_scale(x_ref[...], gain_ref[...]).astype(BF16)

    acc = _dot(h_ref[...], w_ref[...])

    def rotate(xh, half):
        return (xh * tab_ref[0, 0]
                + pltpu.roll(xh, LANES - half, 1) * tab_ref[0, 1]
                + pltpu.roll(xh, half, 1) * tab_ref[0, 2])

    def emit(out_ref, fn):
        for hh in range(6):
            out_ref[hh] = fn(acc[:, hh * HEAD_DIM:(hh + 1) * HEAD_DIM]).astype(out_ref.dtype)

    @pl.when(j < 2)
    def _():
        emit(oa_ref, lambda xh: rotate(xh, HEAD_DIM // ROPE_FRACTION // 2))

    @pl.when(j == 2)
    def _():
        emit(oa_ref, lambda xh: xh)

    @pl.when((j >= 3) & (j < 5))
    def _():
        emit(ob_ref, lambda xh: rotate(xh, DIFF_QK_DIM // ROPE_FRACTION // 2))

    @pl.when(j == 5)
    def _():
        emit(ob_ref, lambda xh: xh)


def _in_proj(x, gain, w_attn, tables, *, tm):
    s = x.shape[0]
    grid = (s // tm, 6)
    return pl.pallas_call(
        _in_proj_kernel,
        grid=grid,
        in_specs=[
            pl.BlockSpec((tm, D_MODEL), lambda i, j: (i, 0)),
            pl.BlockSpec((1, D_MODEL), lambda i, j: (0, 0)),
            pl.BlockSpec((D_MODEL, WIDTH_A), lambda i, j: (0, j)),
            pl.BlockSpec((1, 3, tm, LANES), lambda i, j: (j // 3, 0, i, 0)),
        ],
        out_specs=[
            pl.BlockSpec((6, tm, LANES), lambda i, j: (jnp.minimum(j, 2), i, 0)),
            pl.BlockSpec((6, tm, LANES), lambda i, j: (jnp.maximum(j - 3, 0), i, 0)),
        ],
        out_shape=[
            jax.ShapeDtypeStruct((18, s, LANES), F32),
            jax.ShapeDtypeStruct((18, s, LANES), BF16),
        ],
        scratch_shapes=[pltpu.VMEM((tm, D_MODEL), BF16)],
        compiler_params=_params(("arbitrary", "arbitrary")),
        name="in_proj",
    )(x, gain, w_attn, tables)


def _dilated_kernel(q_ref, k_ref, kp_ref, v_ref, vp_ref, o_ref, m_ref, l_ref, acc_ref):
    row = lax.broadcasted_iota(jnp.int32, (BLK, BLK), 0)
    col = lax.broadcasted_iota(jnp.int32, (BLK, BLK), 1)
    cur_mask = col <= row
    prev_mask = col >= row
    first_prev_mask = col >= row + jnp.where(pl.program_id(1) > 0, 0, BLK)
    scale = HEAD_DIM ** -0.5 * LOG2E

    for d in DILATIONS:
        nblk = SUPER // (BLK * d)
        for r in range(d):
            k_prev = v_prev = None
            for bb in range(nblk):
                def rows(start):
                    return pl.ds(start, BLK) if d == 1 else pl.ds(start, BLK, stride=d)

                sl = rows(r + d * BLK * bb)
                q = (q_ref[0, sl, :] * scale).astype(BF16)
                k_cur = k_ref[0, sl, :].astype(BF16)
                v_cur = v_ref[0, sl, :].astype(BF16)
                if bb == 0:
                    psl = rows(SUPER - BLK * d + r)
                    k_prev = kp_ref[0, psl, :].astype(BF16)
                    v_prev = vp_ref[0, psl, :].astype(BF16)
                s_cur = jnp.where(cur_mask, _dot_nt(q, k_cur), MASKED)
                s_prev = jnp.where(first_prev_mask if bb == 0 else prev_mask,
                                   _dot_nt(q, k_prev), MASKED)
                m_blk = jnp.maximum(jnp.max(s_cur, axis=1, keepdims=True),
                                    jnp.max(s_prev, axis=1, keepdims=True))
                m_blk = jnp.broadcast_to(m_blk, (BLK, LANES))
                if d == DILATIONS[0]:
                    m_new = m_blk
                else:
                    m_old = m_ref[sl, :]
                    m_new = jnp.maximum(m_old, m_blk)
                    alpha = jnp.exp2(m_old - m_new)
                p_cur = jnp.exp2(s_cur - m_new)
                p_prev = jnp.exp2(s_prev - m_new)
                l_blk = jnp.broadcast_to(
                    jnp.sum(p_cur, axis=1, keepdims=True) + jnp.sum(p_prev, axis=1, keepdims=True),
                    (BLK, LANES))
                pv = _dot(p_cur.astype(BF16), v_cur) + _dot(p_prev.astype(BF16), v_prev)
                if d == DILATIONS[0]:
                    l_ref[sl, :] = l_blk
                    acc_ref[sl, :] = pv
                else:
                    l_ref[sl, :] = alpha * l_ref[sl, :] + l_blk
                    acc_ref[sl, :] = alpha * acc_ref[sl, :] + pv
                m_ref[sl, :] = m_new
                k_prev, v_prev = k_cur, v_cur

    o_ref[...] = (acc_ref[...] / l_ref[...]).astype(o_ref.dtype)


def _dilated_mixture(slabs_a):
    s = slabs_a.shape[1]

    def prev(sb):
        return jnp.maximum(sb - 1, 0)

    blk = (1, SUPER, LANES)
    return pl.pallas_call(
        _dilated_kernel,
        grid=(N_HEADS_A, s // SUPER),
        in_specs=[
            pl.BlockSpec(blk, lambda h, sb: (h, sb, 0)),
            pl.BlockSpec(blk, lambda h, sb: (N_HEADS_A + h, sb, 0)),
            pl.BlockSpec(blk, lambda h, sb: (N_HEADS_A + h, prev(sb), 0)),
            pl.BlockSpec(blk, lambda h, sb: (2 * N_HEADS_A + h, sb, 0)),
            pl.BlockSpec(blk, lambda h, sb: (2 * N_HEADS_A + h, prev(sb), 0)),
        ],
        out_specs=pl.BlockSpec((SUPER, HEAD_DIM), lambda h, sb: (sb, h)),
        out_shape=jax.ShapeDtypeStruct((s, WIDTH_A), BF16),
        scratch_shapes=[pltpu.VMEM((SUPER, LANES), F32)] * 3,
        compiler_params=_params(("arbitrary", "arbitrary")),
        name="dilated",
    )(slabs_a, slabs_a, slabs_a, slabs_a, slabs_a)


def _diff_attn_kernel(lam_ref, gain_ref, q_ref, k_ref, v_ref, o_ref,
                      qs_ref, m_ref, l_ref, acc_ref, *, t, lambda_init):
    qi = pl.program_id(1)
    q = q_ref[0].astype(F32) * (DIFF_QK_DIM ** -0.5 * LOG2E)
    lane = lax.broadcasted_iota(jnp.int32, (t, HEAD_DIM), 1)
    qs_ref[0:t] = jnp.where(lane < DIFF_QK_DIM, q, 0.0).astype(BF16)
    qs_ref[t:2 * t] = jnp.where(lane >= DIFF_QK_DIM, q, 0.0).astype(BF16)
    m_ref[...] = jnp.full(m_ref.shape, MASKED, F32)
    l_ref[...] = jnp.zeros(l_ref.shape, F32)
    acc_ref[...] = jnp.zeros(acc_ref.shape, F32)

    def chunk(c, diagonal):
        kc = k_ref[0, pl.ds(pl.multiple_of(c * t, t), t), :]
        vc = v_ref[0, pl.ds(pl.multiple_of(c * t, t), t), :]
        s = _dot_nt(qs_ref[...], kc)
        if diagonal:
            row = lax.broadcasted_iota(jnp.int32, (2 * t, t), 0)
            col = lax.broadcasted_iota(jnp.int32, (2 * t, t), 1)
            s = jnp.where(col <= jnp.where(row >= t, row - t, row), s, MASKED)
        m_old = m_ref[...]
        m_new = jnp.maximum(m_old, jnp.max(s, axis=1, keepdims=True))
        alpha = jnp.exp2(m_old - m_new)
        p = jnp.exp2(s - m_new)
        l_ref[...] = alpha * l_ref[...] + jnp.sum(p, axis=1, keepdims=True)
        acc_ref[...] = alpha * acc_ref[...] + _dot(p.astype(BF16), vc)
        m_ref[...] = m_new

    def body(c, carry):
        chunk(c, False)
        return carry

    lax.fori_loop(0, qi, body, 0)
    chunk(qi, True)

    o = acc_ref[...] / l_ref[...]
    lam = (jnp.exp(jnp.sum(lam_ref[0:1, :] * lam_ref[1:2, :], axis=1, keepdims=True))
           - jnp.exp(jnp.sum(lam_ref[2:3, :] * lam_ref[3:4, :], axis=1, keepdims=True))
           + lambda_init)
    ob = o[0:t] - lam * o[t:2 * t]
    o_ref[...] = (_rms_scale(ob, gain_ref[...]) * (1.0 - lambda_init)).astype(o_ref.dtype)


def _diff_attention(slabs_b, lam_vecs, gain, *, t, lambda_init):
    s = slabs_b.shape[1]
    kernel = functools.partial(_diff_attn_kernel, t=t, lambda_init=lambda_init)
    return pl.pallas_call(
        kernel,
        grid=(N_HEADS_B, s // t),
        in_specs=[
            pl.BlockSpec((4, DIFF_QK_DIM), lambda h, qi: (0, 0)),
            pl.BlockSpec((1, HEAD_DIM), lambda h, qi: (0, 0)),
            pl.BlockSpec((1, t, LANES), lambda h, qi: (h, qi, 0)),
            pl.BlockSpec((1, s, LANES), lambda h, qi: (N_HEADS_B + h, 0, 0)),
            pl.BlockSpec((1, s, LANES), lambda h, qi: (2 * N_HEADS_B + h, 0, 0)),
        ],
        out_specs=pl.BlockSpec((t, HEAD_DIM), lambda h, qi: (qi, h)),
        out_shape=jax.ShapeDtypeStruct((s, WIDTH_B), BF16),
        scratch_shapes=[
            pltpu.VMEM((2 * t, HEAD_DIM), BF16),
            pltpu.VMEM((2 * t, 1), F32),
            pltpu.VMEM((2 * t, 1), F32),
            pltpu.VMEM((2 * t, HEAD_DIM), F32),
        ],
        compiler_params=_params(("arbitrary", "arbitrary")),
        name="diff_attn",
    )(lam_vecs, gain, slabs_b, slabs_b, slabs_b)


def _sgu_kernel(x_ref, gain_ref, w_ref, lng_ref, lnb_ref, ws_ref, bs_ref, o_ref, *, tm):
    h = _rms_scale(x_ref[...], gain_ref[...]).astype(BF16)
    uv = _dot(h, w_ref[...])
    uv = 0.5 * uv * (1.0 + lax.erf(uv * math.sqrt(0.5)))
    u = uv[:, :WIDTH_C]
    v = uv[:, WIDTH_C:]
    mu = jnp.mean(v, axis=-1, keepdims=True)
    var = jnp.mean(jnp.square(v - mu), axis=-1, keepdims=True)
    vn = ((v - mu) * lax.rsqrt(var + EPS) * lng_ref[...] + lnb_ref[...]).astype(BF16)
    row = lax.broadcasted_iota(jnp.int32, (CHUNK, CHUNK), 0)
    col = lax.broadcasted_iota(jnp.int32, (CHUNK, CHUNK), 1)
    for g in range(N_HEADS_C):
        wm = jnp.where(col <= row, ws_ref[g], 0.0).astype(BF16)
        bias = bs_ref[:, g:g + 1]
        cols = slice(g * HEAD_DIM, (g + 1) * HEAD_DIM)
        for c in range(tm // CHUNK):
            rows = slice(c * CHUNK, (c + 1) * CHUNK)
            y = _dot(wm, vn[rows, cols]) + bias
            o_ref[rows, cols] = (u[rows, cols] * y).astype(o_ref.dtype)


def _spatial_gating(x, gain, w_uv, ln_g, ln_b, w_s, b_s_t, *, tm):
    s = x.shape[0]
    const2 = lambda i: (0, 0)
    return pl.pallas_call(
        functools.partial(_sgu_kernel, tm=tm),
        grid=(s // tm,),
        in_specs=[
            pl.BlockSpec((tm, D_MODEL), lambda i: (i, 0)),
            pl.BlockSpec((1, D_MODEL), const2),
            pl.BlockSpec((D_MODEL, 2 * WIDTH_C), const2),
            pl.BlockSpec((1, WIDTH_C), const2),
            pl.BlockSpec((1, WIDTH_C), const2),
            pl.BlockSpec((N_HEADS_C, CHUNK, CHUNK), lambda i: (0, 0, 0)),
            pl.BlockSpec((CHUNK, N_HEADS_C), const2),
        ],
        out_specs=pl.BlockSpec((tm, WIDTH_C), lambda i: (i, 0)),
        out_shape=jax.ShapeDtypeStruct((s, WIDTH_C), BF16),
        compiler_params=_params(("arbitrary",)),
        name="sgu",
    )(x, gain, w_uv, ln_g, ln_b, w_s, b_s_t)


def _out_proj_kernel(x_ref, a_ref, b_ref, c_ref, wa_ref, wb_ref, wc_ref, o_ref):
    o_ref[...] = (x_ref[...] + _dot(a_ref[...], wa_ref[...]) + _dot(b_ref[...], wb_ref[...])
                  + _dot(c_ref[...], wc_ref[...]))


def _out_proj(x, a, b, c, w_out, *, tm):
    s = x.shape[0]
    return pl.pallas_call(
        _out_proj_kernel,
        grid=(s // tm,),
        in_specs=[
            pl.BlockSpec((tm, D_MODEL), lambda i: (i, 0)),
            pl.BlockSpec((tm, WIDTH_A), lambda i: (i, 0)),
            pl.BlockSpec((tm, WIDTH_B), lambda i: (i, 0)),
            pl.BlockSpec((tm, WIDTH_C), lambda i: (i, 0)),
            pl.BlockSpec((WIDTH_A, D_MODEL), lambda i: (0, 0)),
            pl.BlockSpec((WIDTH_B, D_MODEL), lambda i: (1, 0)),
            pl.BlockSpec((WIDTH_C, D_MODEL), lambda i: ((WIDTH_A + WIDTH_B) // WIDTH_C, 0)),
        ],
        out_specs=pl.BlockSpec((tm, D_MODEL), lambda i: (i, 0)),
        out_shape=jax.ShapeDtypeStruct((s, D_MODEL), F32),
        compiler_params=_params(("arbitrary",)),
        name="out_proj",
    )(x, a, b, c, w_out, w_out, w_out)


def _ffn_kernel(x_ref, gain_ref, wg_ref, wv_ref, cwg_ref, cwv_ref, cbg_ref, cbv_ref, wd_ref,
                fgain_ref, o_ref, h_ref, ug_ref, uv_ref, halo_g_ref, halo_v_ref,
                *, tm, final_norm):
    i = pl.program_id(0)
    f = pl.program_id(1)

    @pl.when(f == 0)
    def _():
        h_ref[...] = _rms_scale(x_ref[...], gain_ref[...]).astype(BF16)

    @pl.when(i == 0)
    def _():
        halo_g_ref[f] = jnp.zeros(halo_g_ref.shape[1:], F32)
        halo_v_ref[f] = jnp.zeros(halo_v_ref.shape[1:], F32)

    def conv(u_ref, halo_ref, w_up_ref, cw_ref, cb_ref):
        u_ref[0:SUBLANES, :] = halo_ref[f]
        u_ref[SUBLANES:SUBLANES + tm, :] = _dot(h_ref[...], w_up_ref[...])
        halo_ref[f] = u_ref[tm:tm + SUBLANES, :]
        out = cb_ref[...]
        for tap in range(CONV_WIDTH):
            lo = SUBLANES - (CONV_WIDTH - 1) + tap
            out = out + cw_ref[tap:tap + 1, :] * u_ref[lo:lo + tm, :]
        return out

    gate = conv(ug_ref, halo_g_ref, wg_ref, cwg_ref, cbg_ref)
    val = conv(uv_ref, halo_v_ref, wv_ref, cwv_ref, cbv_ref)
    act = (gate * jax.nn.sigmoid(gate) * val).astype(BF16)
    down = _dot(act, wd_ref[...])

    @pl.when(f == 0)
    def _():
        o_ref[...] = x_ref[...] + down

    @pl.when(f > 0)
    def _():
        o_ref[...] += down

    if final_norm:
        @pl.when(f == pl.num_programs(1) - 1)
        def _():
            o_ref[...] = _rms_scale(o_ref[...], fgain_ref[...])


def _ffn(x, gain, w_up, conv_w, conv_b, w_down, final_gain, *, tm, tf, final_norm):
    s = x.shape[0]
    nf = FFN_DIM // tf
    kernel = functools.partial(_ffn_kernel, tm=tm, final_norm=final_norm)
    return pl.pallas_call(
        kernel,
        grid=(s // tm, nf),
        in_specs=[
            pl.BlockSpec((tm, D_MODEL), lambda i, f: (i, 0)),
            pl.BlockSpec((1, D_MODEL), lambda i, f: (0, 0)),
            pl.BlockSpec((D_MODEL, tf), lambda i, f: (0, f)),
            pl.BlockSpec((D_MODEL, tf), lambda i, f: (0, nf + f)),
            pl.BlockSpec((CONV_WIDTH, tf), lambda i, f: (0, f)),
            pl.BlockSpec((CONV_WIDTH, tf), lambda i, f: (0, nf + f)),
            pl.BlockSpec((1, tf), lambda i, f: (0, f)),
            pl.BlockSpec((1, tf), lambda i, f: (0, nf + f)),
            pl.BlockSpec((tf, D_MODEL), lambda i, f: (f, 0)),
            pl.BlockSpec((1, D_MODEL), lambda i, f: (0, 0)),
        ],
        out_specs=pl.BlockSpec((tm, D_MODEL), lambda i, f: (i, 0)),
        out_shape=jax.ShapeDtypeStruct((s, D_MODEL), F32),
        scratch_shapes=[
            pltpu.VMEM((tm, D_MODEL), BF16),
            pltpu.VMEM((tm + SUBLANES, tf), F32),
            pltpu.VMEM((tm + SUBLANES, tf), F32),
            pltpu.VMEM((nf, SUBLANES, tf), F32),
            pltpu.VMEM((nf, SUBLANES, tf), F32),
        ],
        compiler_params=_params(("arbitrary", "arbitrary")),
        name="ffn",
    )(x, gain, w_up, w_up, conv_w, conv_w, conv_b, conv_b, w_down, final_gain)


def _rope_tables(seq):
    def tables(width, reps):
        rot = width // ROPE_FRACTION
        half = rot // 2
        inv = 1.0 / (ROPE_THETA ** (jnp.arange(0, rot, 2, dtype=F32) / rot))
        ang = jnp.arange(seq, dtype=F32)[:, None] * inv[None, :]
        cos, sin = jnp.cos(ang), jnp.sin(ang)
        zeros = lambda n: jnp.zeros((seq, n), F32)
        keep = jnp.concatenate([cos, cos, jnp.ones((seq, width - rot), F32)], axis=1)
        plus = jnp.concatenate([-sin, zeros(width - half)], axis=1)
        minus = jnp.concatenate([zeros(half), sin, zeros(width - rot)], axis=1)
        return jnp.stack([jnp.tile(t, (1, reps)) for t in (keep, plus, minus)])

    return jnp.stack([tables(HEAD_DIM, 1), tables(DIFF_QK_DIM, 2)])


def kernel(x, norm_mix, w_in, lambda_q1, lambda_k1, lambda_q2, lambda_k2, diff_subln,
           sgu_ln_g, sgu_ln_b, sgu_w, sgu_b, w_out, norm_ffn, w_up, conv_w, conv_b, w_down,
           norm_final):
    batch, seq, _ = x.shape
    depth = w_in.shape[0]
    tables = _rope_tables(seq)
    row = lambda v: v.reshape(1, -1)
    outs = []
    for bi in range(batch):
        xs = x[bi]
        for l in range(depth):
            lambda_init = 0.8 - 0.6 * math.exp(-0.3 * l)
            w_in_l = w_in[l].astype(BF16)
            slabs_a, slabs_b = _in_proj(xs, row(norm_mix[l]), w_in_l[:, :ATTN_COLS], tables, tm=1024)
            out_a = _dilated_mixture(slabs_a)
            lam_vecs = jnp.stack([lambda_q1[l], lambda_k1[l], lambda_q2[l], lambda_k2[l]])
            out_b = _diff_attention(slabs_b, lam_vecs, row(diff_subln[l]), t=512,
                                    lambda_init=lambda_init)
            out_c = _spatial_gating(xs, row(norm_mix[l]), w_in_l[:, ATTN_COLS:],
                                    row(sgu_ln_g[l]), row(sgu_ln_b[l]), sgu_w[l],
                                    jnp.transpose(sgu_b[l]), tm=512)
            xs = _out_proj(xs, out_a, out_b, out_c, w_out[l].astype(BF16), tm=512)
            xs = _ffn(xs, row(norm_ffn[l]), w_up[l].astype(BF16), conv_w[l], row(conv_b[l]),
                      w_down[l].astype(BF16), row(norm_final), tm=512, tf=512,
                      final_norm=(l == depth - 1))
        outs.append(xs)
    return jnp.stack(outs)
```

```python
import functools
import math

import jax
import jax.numpy as jnp
from jax import lax
from jax.experimental import pallas as pl
from jax.experimental.pallas import tpu as pltpu

F32 = jnp.float32
BF16 = jnp.bfloat16

D_MODEL = 2048
HEAD_DIM = 128
N_HEADS_A = 6
N_HEADS_B = 6
N_HEADS_C = 4
DIFF_QK_DIM = 64
WIDTH_A = N_HEADS_A * HEAD_DIM
WIDTH_B = N_HEADS_B * HEAD_DIM
WIDTH_C = N_HEADS_C * HEAD_DIM
ATTN_COLS = 3 * WIDTH_A + 3 * WIDTH_B
DILATIONS = (1, 4, 16)
BLK = 128
SUPER = BLK * DILATIONS[-1]
CHUNK = 128
ROPE_THETA = 500000.0
ROPE_FRACTION = 4
FFN_DIM = 5632
CONV_WIDTH = 3
EPS = 1e-6
LOG2E = math.log2(math.e)
MASKED = -1e30

LANES = 128
SUBLANES = 8
VMEM_LIMIT = 56 * 1024 * 1024
DIFF_T = 1024
DIFF_KEYS = 256
DENOM_ROWS = 16
DILATED_SOFTMAX_LAG = 4
DILATED_VALUE_LAG = 8


def _params(semantics, vmem=VMEM_LIMIT):
    return pltpu.CompilerParams(dimension_semantics=semantics, vmem_limit_bytes=vmem)


def _rms_scale(x, gain):
    return x * lax.rsqrt(jnp.mean(x * x, axis=-1, keepdims=True) + EPS) * gain


def _dot_nt(a, b):
    return lax.dot_general(a, b, (((1,), (1,)), ((), ())), preferred_element_type=F32)


def _dot(a, b):
    return jnp.dot(a, b, preferred_element_type=F32)


def _in_proj_kernel(x_ref, gain_ref, w_ref, tab_ref, oa_ref, obk_ref, obt_ref, h_ref, *, t):
    j = pl.program_id(1)

    @pl.when(j == 0)
    def _():
        h_ref[...] = _rms_scale(x_ref[...], gain_ref[...]).astype(BF16)

    acc = _dot(h_ref[...], w_ref[...])

    def rotate(xh, half):
        return (xh * tab_ref[0, 0]
                + pltpu.roll(xh, LANES - half, 1) * tab_ref[0, 1]
                + pltpu.roll(xh, half, 1) * tab_ref[0, 2])

    rotate_a = lambda xh: rotate(xh, HEAD_DIM // ROPE_FRACTION // 2)
    rotate_b = lambda xh: rotate(xh, DIFF_QK_DIM // ROPE_FRACTION // 2)

    def emit(out_ref, fn):
        for hh in range(6):
            out_ref[hh] = fn(acc[:, hh * HEAD_DIM:(hh + 1) * HEAD_DIM]).astype(out_ref.dtype)

    def emit_transposed(fn):
        for hh in range(6):
            y = fn(acc[:, hh * HEAD_DIM:(hh + 1) * HEAD_DIM])
            for c in range(y.shape[0] // t):
                obt_ref[hh, c] = y[c * t:(c + 1) * t, :].T.astype(obt_ref.dtype)

    @pl.when(j < 2)
    def _():
        emit(oa_ref, rotate_a)

    @pl.when(j == 2)
    def _():
        emit(oa_ref, lambda xh: xh)

    @pl.when(j == 3)
    def _():
        emit_transposed(rotate_b)

    @pl.when(j == 4)
    def _():
        emit(obk_ref, rotate_b)

    @pl.when(j == 5)
    def _():
        emit_transposed(lambda xh: xh)


def _in_proj(x, gain, w_in, layer, tables, *, tm, t):
    s = x.shape[0]
    grid = (s // tm, 6)
    return pl.pallas_call(
        functools.partial(_in_proj_kernel, t=t),
        grid=grid,
        in_specs=[
            pl.BlockSpec((tm, D_MODEL), lambda i, j: (i, 0)),
            pl.BlockSpec((1, D_MODEL), lambda i, j: (0, 0)),
            pl.BlockSpec((None, D_MODEL, WIDTH_A), lambda i, j: (layer, 0, j)),
            pl.BlockSpec((1, 3, tm, LANES), lambda i, j: (j // 3, 0, i, 0)),
        ],
        out_specs=[
            pl.BlockSpec((6, tm, LANES), lambda i, j: (jnp.minimum(j, 2), i, 0)),
            pl.BlockSpec((6, tm, LANES), lambda i, j: (0, i, 0)),
            pl.BlockSpec((6, tm // t, HEAD_DIM, t), lambda i, j: (j // 5, i, 0, 0)),
        ],
        out_shape=[
            jax.ShapeDtypeStruct((18, s, LANES), F32),
            jax.ShapeDtypeStruct((6, s, LANES), BF16),
            jax.ShapeDtypeStruct((12, s // t, HEAD_DIM, t), BF16),
        ],
        scratch_shapes=[pltpu.VMEM((tm, D_MODEL), BF16)],
        compiler_params=_params(("arbitrary", "arbitrary")),
        name="in_proj",
    )(x, gain, w_in, tables)


def _dilated_kernel(q_ref, k_ref, kp_ref, v_ref, vp_ref, o_ref, m_ref, l_ref, acc_ref):
    row = lax.broadcasted_iota(jnp.int32, (BLK, 2 * BLK), 0)
    col = lax.broadcasted_iota(jnp.int32, (BLK, 2 * BLK), 1)
    window = jnp.where(col < BLK, col - row, row - col + BLK) >= 0
    no_prev = jnp.where(pl.program_id(1) > 0, 0, BLK)
    first_window = jnp.where(col < BLK, col - row - no_prev, row - col + BLK) >= 0
    scale = HEAD_DIM ** -0.5 * LOG2E
    ones = jnp.ones((2 * BLK, LANES), BF16)

    def rows(d, start):
        return pl.ds(start, BLK) if d == 1 else pl.ds(start, BLK, stride=d)

    def stage_scores(unit, carry):
        pi, d, r, bb = unit
        sl = rows(d, r + d * BLK * bb)
        q = (q_ref[0, sl, :] * scale).astype(BF16)
        k_cur = k_ref[0, sl, :].astype(BF16)
        v_cur = v_ref[0, sl, :].astype(BF16)
        if bb == 0:
            psl = rows(d, SUPER - BLK * d + r)
            k_prev = kp_ref[0, psl, :].astype(BF16)
            v_prev = vp_ref[0, psl, :].astype(BF16)
        else:
            k_prev, v_prev = carry
        scores = _dot_nt(q, jnp.concatenate([k_prev, k_cur], axis=0))
        scores = jnp.where(first_window if bb == 0 else window, scores, MASKED)
        values = jnp.concatenate([jnp.concatenate([v_prev, v_cur], axis=0), ones], axis=1)
        return dict(pi=pi, sl=sl, scores=scores, values=values), (k_cur, v_cur)

    def stage_softmax(state):
        m_blk = jnp.max(state["scores"], axis=1, keepdims=True)
        state["probs"] = jnp.exp2(state.pop("scores") - m_blk).astype(BF16)
        m_ref[state["pi"], state["sl"], :] = jnp.broadcast_to(m_blk, (BLK, LANES))

    def stage_values(state):
        out = _dot(state["probs"], state["values"])
        acc_ref[state["pi"], state["sl"], :] = out[:, :HEAD_DIM]
        l_ref[state["pi"], state["sl"], :] = out[:, HEAD_DIM:]

    units = [(pi, d, r, bb) for pi, d in enumerate(DILATIONS) for r in range(d)
             for bb in range(SUPER // (BLK * d))]
    states = [None] * len(units)
    carry = None
    for i in range(len(units) + DILATED_VALUE_LAG):
        if i < len(units):
            states[i], carry = stage_scores(units[i], carry)
        if 0 <= i - DILATED_SOFTMAX_LAG < len(units):
            stage_softmax(states[i - DILATED_SOFTMAX_LAG])
        if 0 <= i - DILATED_VALUE_LAG < len(units):
            stage_values(states[i - DILATED_VALUE_LAG])
            states[i - DILATED_VALUE_LAG] = None

    m_all = functools.reduce(jnp.maximum, [m_ref[pi] for pi in range(len(DILATIONS))])
    num = den = 0.0
    for pi in range(len(DILATIONS)):
        w = jnp.exp2(m_ref[pi] - m_all)
        num = num + w * acc_ref[pi]
        den = den + w * l_ref[pi]
    o_ref[...] = (num / den).astype(o_ref.dtype)


def _dilated_mixture(slabs_a):
    s = slabs_a.shape[1]

    def prev(sb):
        return jnp.maximum(sb - 1, 0)

    blk = (1, SUPER, LANES)
    return pl.pallas_call(
        _dilated_kernel,
        grid=(N_HEADS_A, s // SUPER),
        in_specs=[
            pl.BlockSpec(blk, lambda h, sb: (h, sb, 0)),
            pl.BlockSpec(blk, lambda h, sb: (N_HEADS_A + h, sb, 0)),
            pl.BlockSpec(blk, lambda h, sb: (N_HEADS_A + h, prev(sb), 0)),
            pl.BlockSpec(blk, lambda h, sb: (2 * N_HEADS_A + h, sb, 0)),
            pl.BlockSpec(blk, lambda h, sb: (2 * N_HEADS_A + h, prev(sb), 0)),
        ],
        out_specs=pl.BlockSpec((SUPER, HEAD_DIM), lambda h, sb: (sb, h)),
        out_shape=jax.ShapeDtypeStruct((s, WIDTH_A), BF16),
        scratch_shapes=[pltpu.VMEM((len(DILATIONS), SUPER, LANES), F32)] * 3,
        compiler_params=_params(("arbitrary", "arbitrary")),
        name="dilated",
    )(slabs_a, slabs_a, slabs_a, slabs_a, slabs_a)


def _diff_attn_kernel(lam_ref, gain_ref, qt_ref, k_ref, vt_ref, o_ref,
                      qs_ref, m_ref, acc_ref, *, t, lambda_init):
    qi = pl.program_id(1)
    kb = DIFF_KEYS
    nsub = t // kb
    qt = qt_ref[0, 0].astype(F32) * (DIFF_QK_DIM ** -0.5 * LOG2E)
    dim = lax.broadcasted_iota(jnp.int32, (HEAD_DIM, kb), 0)
    for b in range(nsub):
        blk = qt[:, b * kb:(b + 1) * kb]
        qs_ref[:, 2 * b * kb:(2 * b + 1) * kb] = jnp.where(dim < DIFF_QK_DIM, blk, 0.0).astype(BF16)
        qs_ref[:, (2 * b + 1) * kb:(2 * b + 2) * kb] = (
            jnp.where(dim >= DIFF_QK_DIM, blk, 0.0).astype(BF16))
    m_ref[...] = jnp.full(m_ref.shape, MASKED, F32)
    acc_ref[...] = jnp.zeros(acc_ref.shape, F32)
    ones = jnp.ones((DENOM_ROWS, kb), BF16)

    def chunk(c, diagonal):
        def first_col(j):
            return 2 * j * kb if diagonal else 0

        def scores(j):
            kc = k_ref[0, pl.ds(pl.multiple_of(c * t + j * kb, kb), kb), :]
            return _dot(kc, qs_ref[:, first_col(j):])

        pending = scores(0)
        for j in range(nsub):
            st = pending
            if j + 1 < nsub:
                pending = scores(j + 1)
            lo = first_col(j)
            if diagonal:
                head = st[:, :2 * kb]
                key = lax.broadcasted_iota(jnp.int32, head.shape, 0)
                col = lax.broadcasted_iota(jnp.int32, head.shape, 1)
                head = jnp.where(key <= jnp.where(col >= kb, col - kb, col), head, MASKED)
                st = head if j == nsub - 1 else jnp.concatenate([head, st[:, 2 * kb:]], axis=1)
            m_old = m_ref[:, lo:]
            m_new = jnp.maximum(m_old, jnp.max(st, axis=0, keepdims=True))
            alpha = jnp.exp2(m_old - m_new)
            p = jnp.exp2(st - m_new).astype(BF16)
            vt_aug = jnp.concatenate([vt_ref[0, c, :, j * kb:(j + 1) * kb], ones], axis=0)
            acc_ref[:, lo:] = alpha * acc_ref[:, lo:] + _dot(vt_aug, p)
            m_ref[:, lo:] = m_new

    def body(c, carry):
        chunk(c, False)
        return carry

    lax.fori_loop(0, qi, body, 0)
    chunk(qi, True)

    o = acc_ref[0:HEAD_DIM, :] / acc_ref[HEAD_DIM:HEAD_DIM + 1, :]
    o1 = jnp.concatenate([o[:, 2 * b * kb:(2 * b + 1) * kb] for b in range(nsub)], axis=1)
    o2 = jnp.concatenate([o[:, (2 * b + 1) * kb:(2 * b + 2) * kb] for b in range(nsub)], axis=1)
    lam = (jnp.exp(jnp.sum(lam_ref[0:1, :] * lam_ref[1:2, :], axis=1, keepdims=True))
           - jnp.exp(jnp.sum(lam_ref[2:3, :] * lam_ref[3:4, :], axis=1, keepdims=True))
           + lambda_init)
    ob = o1 - lam * o2
    ob = ob * lax.rsqrt(jnp.mean(ob * ob, axis=0, keepdims=True) + EPS)
    o_ref[...] = (ob.T * gain_ref[...] * (1.0 - lambda_init)).astype(o_ref.dtype)


def _diff_attention(k_slabs, qv_t, lam_vecs, gain, *, t, lambda_init):
    s = k_slabs.shape[1]
    kernel = functools.partial(_diff_attn_kernel, t=t, lambda_init=lambda_init)
    return pl.pallas_call(
        kernel,
        grid=(N_HEADS_B, s // t),
        in_specs=[
            pl.BlockSpec((4, DIFF_QK_DIM), lambda h, qi: (0, 0)),
            pl.BlockSpec((1, HEAD_DIM), lambda h, qi: (0, 0)),
            pl.BlockSpec((1, 1, HEAD_DIM, t), lambda h, qi: (h, qi, 0, 0)),
            pl.BlockSpec((1, s, LANES), lambda h, qi: (h, 0, 0)),
            pl.BlockSpec((1, s // t, HEAD_DIM, t), lambda h, qi: (N_HEADS_B + h, 0, 0, 0)),
        ],
        out_specs=pl.BlockSpec((t, HEAD_DIM), lambda h, qi: (qi, h)),
        out_shape=jax.ShapeDtypeStruct((s, WIDTH_B), BF16),
        scratch_shapes=[
            pltpu.VMEM((HEAD_DIM, 2 * t), BF16),
            pltpu.VMEM((1, 2 * t), F32),
            pltpu.VMEM((HEAD_DIM + DENOM_ROWS, 2 * t), F32),
        ],
        compiler_params=_params(("arbitrary", "arbitrary")),
        name="diff_attn",
    )(lam_vecs, gain, qv_t, k_slabs, qv_t)


def _sgu_kernel(x_ref, gain_ref, wu_ref, wv_ref, lng_ref, lnb_ref, ws_ref, bs_ref, o_ref, *, tm):
    h = _rms_scale(x_ref[...], gain_ref[...]).astype(BF16)
    gelu = lambda z: 0.5 * z * (1.0 + lax.erf(z * math.sqrt(0.5)))
    u = gelu(_dot(h, wu_ref[...]))
    v = gelu(_dot(h, wv_ref[...]))
    mu = jnp.mean(v, axis=-1, keepdims=True)
    var = jnp.mean(jnp.square(v - mu), axis=-1, keepdims=True)
    vn = ((v - mu) * lax.rsqrt(var + EPS) * lng_ref[...] + lnb_ref[...]).astype(BF16)
    row = lax.broadcasted_iota(jnp.int32, (CHUNK, CHUNK), 0)
    col = lax.broadcasted_iota(jnp.int32, (CHUNK, CHUNK), 1)
    for g in range(N_HEADS_C):
        wm = jnp.where(col <= row, ws_ref[g], 0.0).astype(BF16)
        bias = bs_ref[:, g:g + 1]
        cols = slice(g * HEAD_DIM, (g + 1) * HEAD_DIM)
        for c in range(tm // CHUNK):
            rows = slice(c * CHUNK, (c + 1) * CHUNK)
            y = _dot(wm, vn[rows, cols]) + bias
            o_ref[rows, cols] = (u[rows, cols] * y).astype(o_ref.dtype)


def _spatial_gating(x, gain, w_in, layer, ln_g, ln_b, w_s, b_s_t, *, tm):
    s = x.shape[0]
    const2 = lambda i: (0, 0)
    return pl.pallas_call(
        functools.partial(_sgu_kernel, tm=tm),
        grid=(s // tm,),
        in_specs=[
            pl.BlockSpec((tm, D_MODEL), lambda i: (i, 0)),
            pl.BlockSpec((1, D_MODEL), const2),
            pl.BlockSpec((None, D_MODEL, WIDTH_C), lambda i: (layer, 0, ATTN_COLS // WIDTH_C)),
            pl.BlockSpec((None, D_MODEL, WIDTH_C), lambda i: (layer, 0, ATTN_COLS // WIDTH_C + 1)),
            pl.BlockSpec((1, WIDTH_C), const2),
            pl.BlockSpec((1, WIDTH_C), const2),
            pl.BlockSpec((N_HEADS_C, CHUNK, CHUNK), lambda i: (0, 0, 0)),
            pl.BlockSpec((CHUNK, N_HEADS_C), const2),
        ],
        out_specs=pl.BlockSpec((tm, WIDTH_C), lambda i: (i, 0)),
        out_shape=jax.ShapeDtypeStruct((s, WIDTH_C), BF16),
        compiler_params=_params(("arbitrary",)),
        name="sgu",
    )(x, gain, w_in, w_in, ln_g, ln_b, w_s, b_s_t)


def _out_proj_kernel(x_ref, a_ref, b_ref, c_ref, wa_ref, wb_ref, wc_ref, o_ref):
    o_ref[...] = (x_ref[...] + _dot(a_ref[...], wa_ref[...]) + _dot(b_ref[...], wb_ref[...])
                  + _dot(c_ref[...], wc_ref[...]))


def _out_proj(x, a, b, c, w_out, layer, *, tm):
    s = x.shape[0]
    return pl.pallas_call(
        _out_proj_kernel,
        grid=(s // tm,),
        in_specs=[
            pl.BlockSpec((tm, D_MODEL), lambda i: (i, 0)),
            pl.BlockSpec((tm, WIDTH_A), lambda i: (i, 0)),
            pl.BlockSpec((tm, WIDTH_B), lambda i: (i, 0)),
            pl.BlockSpec((tm, WIDTH_C), lambda i: (i, 0)),
            pl.BlockSpec((None, WIDTH_A, D_MODEL), lambda i: (layer, 0, 0)),
            pl.BlockSpec((None, WIDTH_B, D_MODEL), lambda i: (layer, 1, 0)),
            pl.BlockSpec((None, WIDTH_C, D_MODEL),
                         lambda i: (layer, (WIDTH_A + WIDTH_B) // WIDTH_C, 0)),
        ],
        out_specs=pl.BlockSpec((tm, D_MODEL), lambda i: (i, 0)),
        out_shape=jax.ShapeDtypeStruct((s, D_MODEL), F32),
        compiler_params=_params(("arbitrary",)),
        name="out_proj",
    )(x, a, b, c, w_out, w_out, w_out)


def _ffn_kernel(x_ref, gain_ref, wg_ref, wv_ref, cwg_ref, cwv_ref, cbg_ref, cbv_ref, wd_ref,
                fgain_ref, o_ref, h_ref, ug_ref, uv_ref, halo_g_ref, halo_v_ref,
                *, tm, final_norm):
    i = pl.program_id(0)
    f = pl.program_id(1)

    @pl.when(f == 0)
    def _():
        h_ref[...] = _rms_scale(x_ref[...], gain_ref[...]).astype(BF16)

    @pl.when(i == 0)
    def _():
        halo_g_ref[f] = jnp.zeros(halo_g_ref.shape[1:], F32)
        halo_v_ref[f] = jnp.zeros(halo_v_ref.shape[1:], F32)

    def conv(u_ref, halo_ref, w_up_ref, cw_ref, cb_ref):
        u_ref[0:SUBLANES, :] = halo_ref[f]
        u_ref[SUBLANES:SUBLANES + tm, :] = _dot(h_ref[...], w_up_ref[...])
        halo_ref[f] = u_ref[tm:tm + SUBLANES, :]
        out = cb_ref[...]
        for tap in range(CONV_WIDTH):
            lo = SUBLANES - (CONV_WIDTH - 1) + tap
            out = out + cw_ref[tap:tap + 1, :] * u_ref[lo:lo + tm, :]
        return out

    gate = conv(ug_ref, halo_g_ref, wg_ref, cwg_ref, cbg_ref)
    val = conv(uv_ref, halo_v_ref, wv_ref, cwv_ref, cbv_ref)
    act = (gate * jax.nn.sigmoid(gate) * val).astype(BF16)
    down = _dot(act, wd_ref[...])

    @pl.when(f == 0)
    def _():
        o_ref[...] = x_ref[...] + down

    @pl.when(f > 0)
    def _():
        o_ref[...] += down

    if final_norm:
        @pl.when(f == pl.num_programs(1) - 1)
        def _():
            o_ref[...] = _rms_scale(o_ref[...], fgain_ref[...])


def _ffn(x, gain, w_up, conv_w, conv_b, w_down, layer, final_gain, *, tm, tf, final_norm):
    s = x.shape[0]
    nf = FFN_DIM // tf
    kernel = functools.partial(_ffn_kernel, tm=tm, final_norm=final_norm)
    return pl.pallas_call(
        kernel,
        grid=(s // tm, nf),
        in_specs=[
            pl.BlockSpec((tm, D_MODEL), lambda i, f: (i, 0)),
            pl.BlockSpec((1, D_MODEL), lambda i, f: (0, 0)),
            pl.BlockSpec((None, D_MODEL, tf), lambda i, f: (layer, 0, f)),
            pl.BlockSpec((None, D_MODEL, tf), lambda i, f: (layer, 0, nf + f)),
            pl.BlockSpec((None, CONV_WIDTH, tf), lambda i, f: (layer, 0, f)),
            pl.BlockSpec((None, CONV_WIDTH, tf), lambda i, f: (layer, 0, nf + f)),
            pl.BlockSpec((None, 1, tf), lambda i, f: (layer, 0, f)),
            pl.BlockSpec((None, 1, tf), lambda i, f: (layer, 0, nf + f)),
            pl.BlockSpec((None, tf, D_MODEL), lambda i, f: (layer, f, 0)),
            pl.BlockSpec((1, D_MODEL), lambda i, f: (0, 0)),
        ],
        out_specs=pl.BlockSpec((tm, D_MODEL), lambda i, f: (i, 0)),
        out_shape=jax.ShapeDtypeStruct((s, D_MODEL), F32),
        scratch_shapes=[
            pltpu.VMEM((tm, D_MODEL), BF16),
            pltpu.VMEM((tm + SUBLANES, tf), F32),
            pltpu.VMEM((tm + SUBLANES, tf), F32),
            pltpu.VMEM((nf, SUBLANES, tf), F32),
            pltpu.VMEM((nf, SUBLANES, tf), F32),
        ],
        compiler_params=_params(("arbitrary", "arbitrary")),
        name="ffn",
    )(x, gain, w_up, w_up, conv_w, conv_w, conv_b, conv_b, w_down, final_gain)


def _rope_tables(seq):
    def tables(width, reps):
        rot = width // ROPE_FRACTION
        half = rot // 2
        inv = 1.0 / (ROPE_THETA ** (jnp.arange(0, rot, 2, dtype=F32) / rot))
        ang = jnp.arange(seq, dtype=F32)[:, None] * inv[None, :]
        cos, sin = jnp.cos(ang), jnp.sin(ang)
        zeros = lambda n: jnp.zeros((seq, n), F32)
        keep = jnp.concatenate([cos, cos, jnp.ones((seq, width - rot), F32)], axis=1)
        plus = jnp.concatenate([-sin, zeros(width - half)], axis=1)
        minus = jnp.concatenate([zeros(half), sin, zeros(width - rot)], axis=1)
        return jnp.stack([jnp.tile(t, (1, reps)) for t in (keep, plus, minus)])

    return jnp.stack([tables(HEAD_DIM, 1), tables(DIFF_QK_DIM, 2)])


def kernel(x, norm_mix, w_in, lambda_q1, lambda_k1, lambda_q2, lambda_k2, diff_subln,
           sgu_ln_g, sgu_ln_b, sgu_w, sgu_b, w_out, norm_ffn, w_up, conv_w, conv_b, w_down,
           norm_final):
    batch, seq, _ = x.shape
    depth = w_in.shape[0]
    tables = _rope_tables(seq)
    row = lambda v: v.reshape(1, -1)
    w_in, w_out, w_up, w_down = (w.astype(BF16) for w in (w_in, w_out, w_up, w_down))
    conv_b = conv_b[:, None, :]
    outs = []
    for bi in range(batch):
        xs = x[bi]
        for l in range(depth):
            lambda_init = 0.8 - 0.6 * math.exp(-0.3 * l)
            slabs_a, k_slabs_b, qv_t_b = _in_proj(xs, row(norm_mix[l]), w_in, l, tables,
                                                  tm=1024, t=DIFF_T)
            out_a = _dilated_mixture(slabs_a)
            lam_vecs = jnp.stack([lambda_q1[l], lambda_k1[l], lambda_q2[l], lambda_k2[l]])
            out_b = _diff_attention(k_slabs_b, qv_t_b, lam_vecs, row(diff_subln[l]), t=DIFF_T,
                                    lambda_init=lambda_init)
            out_c = _spatial_gating(xs, row(norm_mix[l]), w_in, l,
                                    row(sgu_ln_g[l]), row(sgu_ln_b[l]), sgu_w[l],
                                    jnp.transpose(sgu_b[l]), tm=512)
            xs = _out_proj(xs, out_a, out_b, out_c, w_out, l, tm=512)
            xs = _ffn(xs, row(norm_ffn[l]), w_up, conv_w, conv_b, w_down, l, row(norm_final),
                      tm=512, tf=512,
                      final_norm=(l == depth - 1))
        outs.append(xs)
    return jnp.stack(outs)
```

```python
import functools
import math

import jax
import jax.numpy as jnp
from jax import lax
from jax.experimental import pallas as pl
from jax.experimental.pallas import tpu as pltpu

F32 = jnp.float32
BF16 = jnp.bfloat16

D_MODEL = 2048
HEAD_DIM = 128
N_HEADS_A = 6
N_HEADS_B = 6
N_HEADS_C = 4
DIFF_QK_DIM = 64
WIDTH_A = N_HEADS_A * HEAD_DIM
WIDTH_B = N_HEADS_B * HEAD_DIM
WIDTH_C = N_HEADS_C * HEAD_DIM
ATTN_COLS = 3 * WIDTH_A + 3 * WIDTH_B
DILATIONS = (1, 4, 16)
BLK = 128
SUPER = BLK * DILATIONS[-1]
CHUNK = 128
ROPE_THETA = 500000.0
ROPE_FRACTION = 4
FFN_DIM = 5632
CONV_WIDTH = 3
EPS = 1e-6
LOG2E = math.log2(math.e)
MASKED = -1e30

LANES = 128
SUBLANES = 8
VMEM_LIMIT = 56 * 1024 * 1024
DIFF_T = 1024
DIFF_KEYS = 256
DIFF_AHEAD = 2
DENOM_ROWS = 16
IN_PROJ_BLOCK = 256
FFN_GROUP = 256
FFN_OUT_BLOCK = 512
DILATED_SOFTMAX_LAG = 2
DILATED_VALUE_LAG = 4


def _params(semantics, vmem=VMEM_LIMIT):
    return pltpu.CompilerParams(dimension_semantics=semantics, vmem_limit_bytes=vmem)


def _rms_scale(x, gain):
    return x * lax.rsqrt(jnp.mean(x * x, axis=-1, keepdims=True) + EPS) * gain


def _dot_nt(a, b):
    return lax.dot_general(a, b, (((1,), (1,)), ((), ())), preferred_element_type=F32)


def _dot(a, b):
    return jnp.dot(a, b, preferred_element_type=F32)


def _in_proj_kernel(x_ref, gain_ref, w_ref, tab_ref, oa_ref, obk_ref, obt_ref, h_ref, *, t):
    j = pl.program_id(1)

    @pl.when(j == 0)
    def _():
        h_ref[...] = _rms_scale(x_ref[...], gain_ref[...]).astype(BF16)

    def rotate(xh, half):
        return (xh * tab_ref[0, 0]
                + pltpu.roll(xh, LANES - half, 1) * tab_ref[0, 1]
                + pltpu.roll(xh, half, 1) * tab_ref[0, 2])

    rotate_a = lambda xh: rotate(xh, HEAD_DIM // ROPE_FRACTION // 2)
    rotate_b = lambda xh: rotate(xh, DIFF_QK_DIM // ROPE_FRACTION // 2)

    identity = lambda xh: xh

    def store(out_ref):
        def fn(hh, y):
            out_ref[hh] = y.astype(out_ref.dtype)
        return fn

    def store_transposed(hh, y):
        for c in range(y.shape[0] // t):
            obt_ref[hh, c] = y[c * t:(c + 1) * t, :].T.astype(obt_ref.dtype)

    def group(fn, put):
        heads_per_block = IN_PROJ_BLOCK // HEAD_DIM
        blocks = [slice(lo, lo + IN_PROJ_BLOCK) for lo in range(0, WIDTH_A, IN_PROJ_BLOCK)]
        project = lambda cols: _dot(h_ref[...], w_ref[:, cols])
        pending = project(blocks[0])
        for bi in range(len(blocks)):
            acc = pending
            if bi + 1 < len(blocks):
                pending = project(blocks[bi + 1])
            for hb in range(heads_per_block):
                put(bi * heads_per_block + hb, fn(acc[:, hb * HEAD_DIM:(hb + 1) * HEAD_DIM]))

    @pl.when(j < 2)
    def _():
        group(rotate_a, store(oa_ref))

    @pl.when(j == 2)
    def _():
        group(identity, store(oa_ref))

    @pl.when(j == 3)
    def _():
        group(rotate_b, store_transposed)

    @pl.when(j == 4)
    def _():
        group(rotate_b, store(obk_ref))

    @pl.when(j == 5)
    def _():
        group(identity, store_transposed)


def _in_proj(x, gain, w_in, layer, tables, *, tm, t):
    s = x.shape[0]
    grid = (s // tm, 6)
    return pl.pallas_call(
        functools.partial(_in_proj_kernel, t=t),
        grid=grid,
        in_specs=[
            pl.BlockSpec((tm, D_MODEL), lambda i, j: (i, 0)),
            pl.BlockSpec((1, D_MODEL), lambda i, j: (0, 0)),
            pl.BlockSpec((None, D_MODEL, WIDTH_A), lambda i, j: (layer, 0, j)),
            pl.BlockSpec((1, 3, tm, LANES), lambda i, j: (j // 3, 0, i, 0)),
        ],
        out_specs=[
            pl.BlockSpec((6, tm, LANES), lambda i, j: (jnp.minimum(j, 2), i, 0)),
            pl.BlockSpec((6, tm, LANES), lambda i, j: (0, i, 0)),
            pl.BlockSpec((6, tm // t, HEAD_DIM, t), lambda i, j: (j // 5, i, 0, 0)),
        ],
        out_shape=[
            jax.ShapeDtypeStruct((18, s, LANES), F32),
            jax.ShapeDtypeStruct((6, s, LANES), BF16),
            jax.ShapeDtypeStruct((12, s // t, HEAD_DIM, t), BF16),
        ],
        scratch_shapes=[pltpu.VMEM((tm, D_MODEL), BF16)],
        compiler_params=_params(("arbitrary", "arbitrary")),
        name="in_proj",
    )(x, gain, w_in, tables)


def _dilated_kernel(q_ref, k_ref, v_ref, o_ref, m_ref, l_ref, acc_ref, kc_ref, vc_ref):
    @pl.when(pl.program_id(1) == 0)
    def _():
        kc_ref[...] = jnp.zeros(kc_ref.shape, BF16)
        vc_ref[...] = jnp.zeros(vc_ref.shape, BF16)

    row = lax.broadcasted_iota(jnp.int32, (BLK, 2 * BLK), 0)
    col = lax.broadcasted_iota(jnp.int32, (BLK, 2 * BLK), 1)
    window = jnp.where(col < BLK, col - row, row - col + BLK) >= 0
    no_prev = jnp.where(pl.program_id(1) > 0, 0, BLK)
    first_window = jnp.where(col < BLK, col - row - no_prev, row - col + BLK) >= 0
    scale = HEAD_DIM ** -0.5 * LOG2E
    ones = jnp.ones((2 * BLK, LANES), BF16)

    def rows(d, start):
        return pl.ds(start, BLK) if d == 1 else pl.ds(start, BLK, stride=d)

    chains = [(d, r) for d in DILATIONS for r in range(d)]

    def stage_scores(unit, carry):
        pi, d, r, bb = unit
        chain = chains.index((d, r))
        sl = rows(d, r + d * BLK * bb)
        q = (q_ref[0, sl, :] * scale).astype(BF16)
        k_cur = k_ref[0, sl, :].astype(BF16)
        v_cur = v_ref[0, sl, :].astype(BF16)
        if bb == 0:
            k_prev, v_prev = kc_ref[chain], vc_ref[chain]
        else:
            k_prev, v_prev = carry
        if bb == SUPER // (BLK * d) - 1:
            kc_ref[chain] = k_cur
            vc_ref[chain] = v_cur
        scores = _dot_nt(q, jnp.concatenate([k_prev, k_cur], axis=0))
        scores = jnp.where(first_window if bb == 0 else window, scores, MASKED)
        values = jnp.concatenate([jnp.concatenate([v_prev, v_cur], axis=0), ones], axis=1)
        return dict(pi=pi, sl=sl, scores=scores, values=values), (k_cur, v_cur)

    def stage_softmax(state):
        m_blk = jnp.max(state["scores"], axis=1, keepdims=True)
        state["probs"] = jnp.exp2(state.pop("scores") - m_blk).astype(BF16)
        m_ref[state["pi"], state["sl"], :] = jnp.broadcast_to(m_blk, (BLK, LANES))

    def stage_values(state):
        out = _dot(state["probs"], state["values"])
        acc_ref[state["pi"], state["sl"], :] = out[:, :HEAD_DIM]
        l_ref[state["pi"], state["sl"], :] = out[:, HEAD_DIM:]

    units = [(pi, d, r, bb) for pi, d in enumerate(DILATIONS) for r in range(d)
             for bb in range(SUPER // (BLK * d))]
    states = [None] * len(units)
    carry = None
    for i in range(len(units) + DILATED_VALUE_LAG):
        if i < len(units):
            states[i], carry = stage_scores(units[i], carry)
        if 0 <= i - DILATED_SOFTMAX_LAG < len(units):
            stage_softmax(states[i - DILATED_SOFTMAX_LAG])
        if 0 <= i - DILATED_VALUE_LAG < len(units):
            stage_values(states[i - DILATED_VALUE_LAG])
            states[i - DILATED_VALUE_LAG] = None

    for lo in range(0, SUPER, BLK):
        sl = slice(lo, lo + BLK)
        m_all = functools.reduce(jnp.maximum, [m_ref[pi, sl, :] for pi in range(len(DILATIONS))])
        num = den = 0.0
        for pi in range(len(DILATIONS)):
            w = jnp.exp2(m_ref[pi, sl, :] - m_all)
            num = num + w * acc_ref[pi, sl, :]
            den = den + w * l_ref[pi, sl, :]
        o_ref[sl, :] = (num / den).astype(o_ref.dtype)


def _dilated_mixture(slabs_a):
    s = slabs_a.shape[1]

    blk = (1, SUPER, LANES)
    return pl.pallas_call(
        _dilated_kernel,
        grid=(N_HEADS_A, s // SUPER),
        in_specs=[
            pl.BlockSpec(blk, lambda h, sb: (h, sb, 0)),
            pl.BlockSpec(blk, lambda h, sb: (N_HEADS_A + h, sb, 0)),
            pl.BlockSpec(blk, lambda h, sb: (2 * N_HEADS_A + h, sb, 0)),
        ],
        out_specs=pl.BlockSpec((SUPER, HEAD_DIM), lambda h, sb: (sb, h)),
        out_shape=jax.ShapeDtypeStruct((s, WIDTH_A), BF16),
        scratch_shapes=([pltpu.VMEM((len(DILATIONS), SUPER, LANES), F32)] * 3
                        + [pltpu.VMEM((sum(DILATIONS), BLK, HEAD_DIM), BF16)] * 2),
        compiler_params=_params(("arbitrary", "arbitrary")),
        name="dilated",
    )(slabs_a, slabs_a, slabs_a)


def _diff_attn_kernel(lam_ref, gain_ref, qt_ref, k_ref, vt_ref, o_ref,
                      qs_ref, m_ref, acc_ref, *, t, lambda_init):
    qi = pl.program_id(1)
    kb = DIFF_KEYS
    nsub = t // kb
    qt = qt_ref[0, 0].astype(F32) * (DIFF_QK_DIM ** -0.5 * LOG2E)
    dim = lax.broadcasted_iota(jnp.int32, (HEAD_DIM, kb), 0)
    for b in range(nsub):
        blk = qt[:, b * kb:(b + 1) * kb]
        qs_ref[:, 2 * b * kb:(2 * b + 1) * kb] = jnp.where(dim < DIFF_QK_DIM, blk, 0.0).astype(BF16)
        qs_ref[:, (2 * b + 1) * kb:(2 * b + 2) * kb] = (
            jnp.where(dim >= DIFF_QK_DIM, blk, 0.0).astype(BF16))
    m_ref[...] = jnp.full(m_ref.shape, MASKED, F32)
    acc_ref[...] = jnp.zeros(acc_ref.shape, F32)
    ones = jnp.ones((DENOM_ROWS, kb), BF16)

    def chunk(c, diagonal):
        def first_col(j):
            return 2 * j * kb if diagonal else 0

        def scores(j):
            kc = k_ref[0, pl.ds(pl.multiple_of(c * t + j * kb, kb), kb), :]
            return _dot(kc, qs_ref[:, first_col(j):])

        pending = [scores(j) for j in range(min(DIFF_AHEAD, nsub))]
        for j in range(nsub):
            st = pending.pop(0)
            if j + DIFF_AHEAD < nsub:
                pending.append(scores(j + DIFF_AHEAD))
            lo = first_col(j)
            if diagonal:
                head = st[:, :2 * kb]
                key = lax.broadcasted_iota(jnp.int32, head.shape, 0)
                col = lax.broadcasted_iota(jnp.int32, head.shape, 1)
                head = jnp.where(key <= jnp.where(col >= kb, col - kb, col), head, MASKED)
                st = head if j == nsub - 1 else jnp.concatenate([head, st[:, 2 * kb:]], axis=1)
            m_old = m_ref[:, lo:]
            m_new = jnp.maximum(m_old, jnp.max(st, axis=0, keepdims=True))
            alpha = jnp.exp2(m_old - m_new)
            p = jnp.exp2(st - m_new).astype(BF16)
            vt_aug = jnp.concatenate([vt_ref[0, c, :, j * kb:(j + 1) * kb], ones], axis=0)
            acc_ref[:, lo:] = alpha * acc_ref[:, lo:] + _dot(vt_aug, p)
            m_ref[:, lo:] = m_new

    def body(c, carry):
        chunk(c, False)
        return carry

    lax.fori_loop(0, qi, body, 0)
    chunk(qi, True)

    o = acc_ref[0:HEAD_DIM, :] / acc_ref[HEAD_DIM:HEAD_DIM + 1, :]
    o1 = jnp.concatenate([o[:, 2 * b * kb:(2 * b + 1) * kb] for b in range(nsub)], axis=1)
    o2 = jnp.concatenate([o[:, (2 * b + 1) * kb:(2 * b + 2) * kb] for b in range(nsub)], axis=1)
    lam = (jnp.exp(jnp.sum(lam_ref[0:1, :] * lam_ref[1:2, :], axis=1, keepdims=True))
           - jnp.exp(jnp.sum(lam_ref[2:3, :] * lam_ref[3:4, :], axis=1, keepdims=True))
           + lambda_init)
    ob = o1 - lam * o2
    ob = ob * lax.rsqrt(jnp.mean(ob * ob, axis=0, keepdims=True) + EPS)
    o_ref[...] = (ob.T * gain_ref[...] * (1.0 - lambda_init)).astype(o_ref.dtype)


def _diff_attention(k_slabs, qv_t, lam_vecs, gain, *, t, lambda_init):
    s = k_slabs.shape[1]
    kernel = functools.partial(_diff_attn_kernel, t=t, lambda_init=lambda_init)
    return pl.pallas_call(
        kernel,
        grid=(N_HEADS_B, s // t),
        in_specs=[
            pl.BlockSpec((4, DIFF_QK_DIM), lambda h, qi: (0, 0)),
            pl.BlockSpec((1, HEAD_DIM), lambda h, qi: (0, 0)),
            pl.BlockSpec((1, 1, HEAD_DIM, t), lambda h, qi: (h, qi, 0, 0)),
            pl.BlockSpec((1, s, LANES), lambda h, qi: (h, 0, 0)),
            pl.BlockSpec((1, s // t, HEAD_DIM, t), lambda h, qi: (N_HEADS_B + h, 0, 0, 0)),
        ],
        out_specs=pl.BlockSpec((t, HEAD_DIM), lambda h, qi: (qi, h)),
        out_shape=jax.ShapeDtypeStruct((s, WIDTH_B), BF16),
        scratch_shapes=[
            pltpu.VMEM((HEAD_DIM, 2 * t), BF16),
            pltpu.VMEM((1, 2 * t), F32),
            pltpu.VMEM((HEAD_DIM + DENOM_ROWS, 2 * t), F32),
        ],
        compiler_params=_params(("arbitrary", "arbitrary")),
        name="diff_attn",
    )(lam_vecs, gain, qv_t, k_slabs, qv_t)


def _sgu_kernel(x_ref, gain_ref, wu_ref, wv_ref, lng_ref, lnb_ref, ws_ref, bs_ref, o_ref, *, tm):
    h = _rms_scale(x_ref[...], gain_ref[...]).astype(BF16)
    gelu = lambda z: 0.5 * z * (1.0 + lax.erf(z * math.sqrt(0.5)))
    u = gelu(_dot(h, wu_ref[...]))
    v = gelu(_dot(h, wv_ref[...]))
    mu = jnp.mean(v, axis=-1, keepdims=True)
    var = jnp.mean(jnp.square(v - mu), axis=-1, keepdims=True)
    vn = ((v - mu) * lax.rsqrt(var + EPS) * lng_ref[...] + lnb_ref[...]).astype(BF16)
    row = lax.broadcasted_iota(jnp.int32, (CHUNK, CHUNK), 0)
    col = lax.broadcasted_iota(jnp.int32, (CHUNK, CHUNK), 1)
    for g in range(N_HEADS_C):
        wm = jnp.where(col <= row, ws_ref[g], 0.0).astype(BF16)
        bias = bs_ref[:, g:g + 1]
        cols = slice(g * HEAD_DIM, (g + 1) * HEAD_DIM)
        for c in range(tm // CHUNK):
            rows = slice(c * CHUNK, (c + 1) * CHUNK)
            y = _dot(wm, vn[rows, cols]) + bias
            o_ref[rows, cols] = (u[rows, cols] * y).astype(o_ref.dtype)


def _spatial_gating(x, gain, w_in, layer, ln_g, ln_b, w_s, b_s_t, *, tm):
    s = x.shape[0]
    const2 = lambda i: (0, 0)
    return pl.pallas_call(
        functools.partial(_sgu_kernel, tm=tm),
        grid=(s // tm,),
        in_specs=[
            pl.BlockSpec((tm, D_MODEL), lambda i: (i, 0)),
            pl.BlockSpec((1, D_MODEL), const2),
            pl.BlockSpec((None, D_MODEL, WIDTH_C), lambda i: (layer, 0, ATTN_COLS // WIDTH_C)),
            pl.BlockSpec((None, D_MODEL, WIDTH_C), lambda i: (layer, 0, ATTN_COLS // WIDTH_C + 1)),
            pl.BlockSpec((1, WIDTH_C), const2),
            pl.BlockSpec((1, WIDTH_C), const2),
            pl.BlockSpec((N_HEADS_C, CHUNK, CHUNK), lambda i: (0, 0, 0)),
            pl.BlockSpec((CHUNK, N_HEADS_C), const2),
        ],
        out_specs=pl.BlockSpec((tm, WIDTH_C), lambda i: (i, 0)),
        out_shape=jax.ShapeDtypeStruct((s, WIDTH_C), BF16),
        compiler_params=_params(("arbitrary",)),
        name="sgu",
    )(x, gain, w_in, w_in, ln_g, ln_b, w_s, b_s_t)


def _out_proj_kernel(x_ref, a_ref, b_ref, c_ref, wa_ref, wb_ref, wc_ref, o_ref):
    o_ref[...] = (x_ref[...] + _dot(a_ref[...], wa_ref[...]) + _dot(b_ref[...], wb_ref[...])
                  + _dot(c_ref[...], wc_ref[...]))


def _out_proj(x, a, b, c, w_out, layer, *, tm):
    s = x.shape[0]
    return pl.pallas_call(
        _out_proj_kernel,
        grid=(s // tm,),
        in_specs=[
            pl.BlockSpec((tm, D_MODEL), lambda i: (i, 0)),
            pl.BlockSpec((tm, WIDTH_A), lambda i: (i, 0)),
            pl.BlockSpec((tm, WIDTH_B), lambda i: (i, 0)),
            pl.BlockSpec((tm, WIDTH_C), lambda i: (i, 0)),
            pl.BlockSpec((None, WIDTH_A, D_MODEL), lambda i: (layer, 0, 0)),
            pl.BlockSpec((None, WIDTH_B, D_MODEL), lambda i: (layer, 1, 0)),
            pl.BlockSpec((None, WIDTH_C, D_MODEL),
                         lambda i: (layer, (WIDTH_A + WIDTH_B) // WIDTH_C, 0)),
        ],
        out_specs=pl.BlockSpec((tm, D_MODEL), lambda i: (i, 0)),
        out_shape=jax.ShapeDtypeStruct((s, D_MODEL), F32),
        compiler_params=_params(("arbitrary",)),
        name="out_proj",
    )(x, a, b, c, w_out, w_out, w_out)


def _ffn_kernel(x_ref, gain_ref, wg_ref, wv_ref, cwg_ref, cwv_ref, cbg_ref, cbv_ref, wd_ref,
                fgain_ref, o_ref, h_ref, halo_g_ref, halo_v_ref, *u_refs, tm, final_norm):
    i = pl.program_id(0)
    f = pl.program_id(1)

    @pl.when(f == 0)
    def _():
        h_ref[...] = _rms_scale(x_ref[...], gain_ref[...]).astype(BF16)
        o_ref[...] = x_ref[...]

    @pl.when(i == 0)
    def _():
        halo_g_ref[f] = jnp.zeros(halo_g_ref.shape[1:], F32)
        halo_v_ref[f] = jnp.zeros(halo_v_ref.shape[1:], F32)

    def up(u_ref, halo_ref, w_up_ref, cols):
        u_ref[0:SUBLANES, :] = halo_ref[f, :, cols]
        u_ref[SUBLANES:SUBLANES + tm, :] = _dot(h_ref[...], w_up_ref[:, cols])
        halo_ref[f, :, cols] = u_ref[tm:tm + SUBLANES, :]

    def conv(u_ref, cw_ref, cb_ref, cols):
        out = cb_ref[:, cols]
        for tap in range(CONV_WIDTH):
            lo = SUBLANES - (CONV_WIDTH - 1) + tap
            out = out + cw_ref[tap:tap + 1, cols] * u_ref[lo:lo + tm, :]
        return out

    tf = wd_ref.shape[0]
    groups = [slice(lo, lo + FFN_GROUP) for lo in range(0, tf, FFN_GROUP)]
    def up_both(g):
        up(u_refs[2 * g], halo_g_ref, wg_ref, groups[g])
        up(u_refs[2 * g + 1], halo_v_ref, wv_ref, groups[g])

    def gated(g):
        gate = conv(u_refs[2 * g], cwg_ref, cbg_ref, groups[g])
        val = conv(u_refs[2 * g + 1], cwv_ref, cbv_ref, groups[g])
        return (gate * jax.nn.sigmoid(gate) * val).astype(BF16)

    def down(g, act):
        for lo in range(0, D_MODEL, FFN_OUT_BLOCK):
            out_cols = slice(lo, lo + FFN_OUT_BLOCK)
            o_ref[:, out_cols] += _dot(act, wd_ref[groups[g], out_cols])

    up_both(0)
    for g in range(len(groups)):
        act = gated(g)
        if g + 1 < len(groups):
            up_both(g + 1)
        down(g, act)

    if final_norm:
        @pl.when(f == pl.num_programs(1) - 1)
        def _():
            o_ref[...] = _rms_scale(o_ref[...], fgain_ref[...])


def _ffn(x, gain, w_up, conv_w, conv_b, w_down, layer, final_gain, *, tm, tf, final_norm):
    s = x.shape[0]
    nf = FFN_DIM // tf
    kernel = functools.partial(_ffn_kernel, tm=tm, final_norm=final_norm)
    return pl.pallas_call(
        kernel,
        grid=(s // tm, nf),
        in_specs=[
            pl.BlockSpec((tm, D_MODEL), lambda i, f: (i, 0)),
            pl.BlockSpec((1, D_MODEL), lambda i, f: (0, 0)),
            pl.BlockSpec((None, D_MODEL, tf), lambda i, f: (layer, 0, f)),
            pl.BlockSpec((None, D_MODEL, tf), lambda i, f: (layer, 0, nf + f)),
            pl.BlockSpec((None, CONV_WIDTH, tf), lambda i, f: (layer, 0, f)),
            pl.BlockSpec((None, CONV_WIDTH, tf), lambda i, f: (layer, 0, nf + f)),
            pl.BlockSpec((None, 1, tf), lambda i, f: (layer, 0, f)),
            pl.BlockSpec((None, 1, tf), lambda i, f: (layer, 0, nf + f)),
            pl.BlockSpec((None, tf, D_MODEL), lambda i, f: (layer, f, 0)),
            pl.BlockSpec((1, D_MODEL), lambda i, f: (0, 0)),
        ],
        out_specs=pl.BlockSpec((tm, D_MODEL), lambda i, f: (i, 0)),
        out_shape=jax.ShapeDtypeStruct((s, D_MODEL), F32),
        scratch_shapes=[
            pltpu.VMEM((tm, D_MODEL), BF16),
            pltpu.VMEM((nf, SUBLANES, tf), F32),
            pltpu.VMEM((nf, SUBLANES, tf), F32),
        ] + [pltpu.VMEM((tm + SUBLANES, FFN_GROUP), F32)] * (2 * tf // FFN_GROUP),
        compiler_params=_params(("arbitrary", "arbitrary")),
        name="ffn",
    )(x, gain, w_up, w_up, conv_w, conv_w, conv_b, conv_b, w_down, final_gain)


def _rope_tables(seq):
    def tables(width, reps):
        rot = width // ROPE_FRACTION
        half = rot // 2
        inv = 1.0 / (ROPE_THETA ** (jnp.arange(0, rot, 2, dtype=F32) / rot))
        ang = jnp.arange(seq, dtype=F32)[:, None] * inv[None, :]
        cos, sin = jnp.cos(ang), jnp.sin(ang)
        zeros = lambda n: jnp.zeros((seq, n), F32)
        keep = jnp.concatenate([cos, cos, jnp.ones((seq, width - rot), F32)], axis=1)
        plus = jnp.concatenate([-sin, zeros(width - half)], axis=1)
        minus = jnp.concatenate([zeros(half), sin, zeros(width - rot)], axis=1)
        return jnp.stack([jnp.tile(t, (1, reps)) for t in (keep, plus, minus)])

    return jnp.stack([tables(HEAD_DIM, 1), tables(DIFF_QK_DIM, 2)])


def kernel(x, norm_mix, w_in, lambda_q1, lambda_k1, lambda_q2, lambda_k2, diff_subln,
           sgu_ln_g, sgu_ln_b, sgu_w, sgu_b, w_out, norm_ffn, w_up, conv_w, conv_b, w_down,
           norm_final):
    batch, seq, _ = x.shape
    depth = w_in.shape[0]
    tables = _rope_tables(seq)
    row = lambda v: v.reshape(1, -1)
    w_in, w_out, w_up, w_down = (w.astype(BF16) for w in (w_in, w_out, w_up, w_down))
    conv_b = conv_b[:, None, :]
    outs = []
    for bi in range(batch):
        xs = x[bi]
        for l in range(depth):
            lambda_init = 0.8 - 0.6 * math.exp(-0.3 * l)
            slabs_a, k_slabs_b, qv_t_b = _in_proj(xs, row(norm_mix[l]), w_in, l, tables,
                                                  tm=1024, t=DIFF_T)
            out_a = _dilated_mixture(slabs_a)
            lam_vecs = jnp.stack([lambda_q1[l], lambda_k1[l], lambda_q2[l], lambda_k2[l]])
            out_b = _diff_attention(k_slabs_b, qv_t_b, lam_vecs, row(diff_subln[l]), t=DIFF_T,
                                    lambda_init=lambda_init)
            out_c = _spatial_gating(xs, row(norm_mix[l]), w_in, l,
                                    row(sgu_ln_g[l]), row(sgu_ln_b[l]), sgu_w[l],
                                    jnp.transpose(sgu_b[l]), tm=512)
            xs = _out_proj(xs, out_a, out_b, out_c, w_out, l, tm=512)
            xs = _ffn(xs, row(norm_ffn[l]), w_up, conv_w, conv_b, w_down, l, row(norm_final),
                      tm=512, tf=512,
                      final_norm=(l == depth - 1))
        outs.append(xs)
    return jnp.stack(outs)
```

```python
import functools
import math

import jax
import jax.numpy as jnp
from jax import lax
from jax.experimental import pallas as pl
from jax.experimental.pallas import tpu as pltpu

F32 = jnp.float32
BF16 = jnp.bfloat16

D_MODEL = 2048
HEAD_DIM = 128
N_HEADS_A = 6
N_HEADS_B = 6
N_HEADS_C = 4
DIFF_QK_DIM = 64
WIDTH_A = N_HEADS_A * HEAD_DIM
WIDTH_B = N_HEADS_B * HEAD_DIM
WIDTH_C = N_HEADS_C * HEAD_DIM
ATTN_COLS = 3 * WIDTH_A + 3 * WIDTH_B
DILATIONS = (1, 4, 16)
BLK = 128
SUPER = BLK * DILATIONS[-1]
CHUNK = 128
ROPE_THETA = 500000.0
ROPE_FRACTION = 4
FFN_DIM = 5632
CONV_WIDTH = 3
EPS = 1e-6
LOG2E = math.log2(math.e)
MASKED = -1e30

LANES = 128
SUBLANES = 8
VMEM_LIMIT = 56 * 1024 * 1024
DIFF_T = 1024
DIFF_KEYS = 256
DIFF_AHEAD = 2
DENOM_ROWS = 16
IN_PROJ_BLOCK = 256
FFN_GROUP = 256
FFN_OUT_BLOCK = 512
DILATED_SOFTMAX_LAG = 2
DILATED_VALUE_LAG = 4


def _params(semantics, vmem=VMEM_LIMIT):
    return pltpu.CompilerParams(dimension_semantics=semantics, vmem_limit_bytes=vmem)


def _rms_scale(x, gain):
    return x * lax.rsqrt(jnp.mean(x * x, axis=-1, keepdims=True) + EPS) * gain


def _dot_nt(a, b):
    return lax.dot_general(a, b, (((1,), (1,)), ((), ())), preferred_element_type=F32)


def _dot(a, b):
    return jnp.dot(a, b, preferred_element_type=F32)


def _in_proj_kernel(x_ref, gain_ref, w_ref, tab_ref, oa_ref, obk_ref, obt_ref, h_ref, *, t):
    j = pl.program_id(1)

    @pl.when(j == 0)
    def _():
        h_ref[...] = _rms_scale(x_ref[...], gain_ref[...]).astype(BF16)

    def rotate(xh, half):
        return (xh * tab_ref[0, 0]
                + pltpu.roll(xh, LANES - half, 1) * tab_ref[0, 1]
                + pltpu.roll(xh, half, 1) * tab_ref[0, 2])

    rotate_a = lambda xh: rotate(xh, HEAD_DIM // ROPE_FRACTION // 2)
    rotate_b = lambda xh: rotate(xh, DIFF_QK_DIM // ROPE_FRACTION // 2)

    identity = lambda xh: xh

    def store(out_ref):
        def fn(hh, y):
            out_ref[hh] = y.astype(out_ref.dtype)
        return fn

    def store_transposed(hh, y):
        for c in range(y.shape[0] // t):
            obt_ref[hh, c] = y[c * t:(c + 1) * t, :].T.astype(obt_ref.dtype)

    def group(fn, put):
        heads_per_block = IN_PROJ_BLOCK // HEAD_DIM
        blocks = [slice(lo, lo + IN_PROJ_BLOCK) for lo in range(0, WIDTH_A, IN_PROJ_BLOCK)]
        project = lambda cols: _dot(h_ref[...], w_ref[:, cols])
        pending = project(blocks[0])
        for bi in range(len(blocks)):
            acc = pending
            if bi + 1 < len(blocks):
                pending = project(blocks[bi + 1])
            for hb in range(heads_per_block):
                put(bi * heads_per_block + hb, fn(acc[:, hb * HEAD_DIM:(hb + 1) * HEAD_DIM]))

    @pl.when(j < 2)
    def _():
        group(rotate_a, store(oa_ref))

    @pl.when(j == 2)
    def _():
        group(identity, store(oa_ref))

    @pl.when(j == 3)
    def _():
        group(rotate_b, store_transposed)

    @pl.when(j == 4)
    def _():
        group(rotate_b, store(obk_ref))

    @pl.when(j == 5)
    def _():
        group(identity, store_transposed)


def _in_proj(x, gain, w_in, tables, *, tm, t):
    s = x.shape[0]
    grid = (s // tm, 6)
    return pl.pallas_call(
        functools.partial(_in_proj_kernel, t=t),
        grid=grid,
        in_specs=[
            pl.BlockSpec((tm, D_MODEL), lambda i, j: (i, 0)),
            pl.BlockSpec((1, D_MODEL), lambda i, j: (0, 0)),
            pl.BlockSpec((D_MODEL, WIDTH_A), lambda i, j: (0, j)),
            pl.BlockSpec((1, 3, tm, LANES), lambda i, j: (j // 3, 0, i, 0)),
        ],
        out_specs=[
            pl.BlockSpec((6, tm, LANES), lambda i, j: (jnp.minimum(j, 2), i, 0)),
            pl.BlockSpec((6, tm, LANES), lambda i, j: (0, i, 0)),
            pl.BlockSpec((6, tm // t, HEAD_DIM, t), lambda i, j: (j // 5, i, 0, 0)),
        ],
        out_shape=[
            jax.ShapeDtypeStruct((18, s, LANES), F32),
            jax.ShapeDtypeStruct((6, s, LANES), BF16),
            jax.ShapeDtypeStruct((12, s // t, HEAD_DIM, t), BF16),
        ],
        scratch_shapes=[pltpu.VMEM((tm, D_MODEL), BF16)],
        compiler_params=_params(("arbitrary", "arbitrary")),
        name="in_proj",
    )(x, gain, w_in, tables)


def _dilated_kernel(q_ref, k_ref, v_ref, o_ref, m_ref, l_ref, acc_ref, kc_ref, vc_ref):
    @pl.when(pl.program_id(1) == 0)
    def _():
        kc_ref[...] = jnp.zeros(kc_ref.shape, BF16)
        vc_ref[...] = jnp.zeros(vc_ref.shape, BF16)

    row = lax.broadcasted_iota(jnp.int32, (BLK, 2 * BLK), 0)
    col = lax.broadcasted_iota(jnp.int32, (BLK, 2 * BLK), 1)
    window = jnp.where(col < BLK, col - row, row - col + BLK) >= 0
    no_prev = jnp.where(pl.program_id(1) > 0, 0, BLK)
    first_window = jnp.where(col < BLK, col - row - no_prev, row - col + BLK) >= 0
    scale = HEAD_DIM ** -0.5 * LOG2E
    ones = jnp.ones((2 * BLK, LANES), BF16)

    def rows(d, start):
        return pl.ds(start, BLK) if d == 1 else pl.ds(start, BLK, stride=d)

    chains = [(d, r) for d in DILATIONS for r in range(d)]

    def stage_scores(unit, carry):
        pi, d, r, bb = unit
        chain = chains.index((d, r))
        sl = rows(d, r + d * BLK * bb)
        q = (q_ref[0, sl, :] * scale).astype(BF16)
        k_cur = k_ref[0, sl, :].astype(BF16)
        v_cur = v_ref[0, sl, :].astype(BF16)
        if bb == 0:
            k_prev, v_prev = kc_ref[chain], vc_ref[chain]
        else:
            k_prev, v_prev = carry
        if bb == SUPER // (BLK * d) - 1:
            kc_ref[chain] = k_cur
            vc_ref[chain] = v_cur
        scores = _dot_nt(q, jnp.concatenate([k_prev, k_cur], axis=0))
        scores = jnp.where(first_window if bb == 0 else window, scores, MASKED)
        values = jnp.concatenate([jnp.concatenate([v_prev, v_cur], axis=0), ones], axis=1)
        return dict(pi=pi, sl=sl, scores=scores, values=values), (k_cur, v_cur)

    def stage_softmax(state):
        m_blk = jnp.max(state["scores"], axis=1, keepdims=True)
        state["probs"] = jnp.exp2(state.pop("scores") - m_blk).astype(BF16)
        m_ref[state["pi"], state["sl"], :] = jnp.broadcast_to(m_blk, (BLK, LANES))

    def stage_values(state):
        out = _dot(state["probs"], state["values"])
        acc_ref[state["pi"], state["sl"], :] = out[:, :HEAD_DIM]
        l_ref[state["pi"], state["sl"], :] = out[:, HEAD_DIM:]

    units = [(pi, d, r, bb) for pi, d in enumerate(DILATIONS) for r in range(d)
             for bb in range(SUPER // (BLK * d))]
    states = [None] * len(units)
    carry = None
    for i in range(len(units) + DILATED_VALUE_LAG):
        if i < len(units):
            states[i], carry = stage_scores(units[i], carry)
        if 0 <= i - DILATED_SOFTMAX_LAG < len(units):
            stage_softmax(states[i - DILATED_SOFTMAX_LAG])
        if 0 <= i - DILATED_VALUE_LAG < len(units):
            stage_values(states[i - DILATED_VALUE_LAG])
            states[i - DILATED_VALUE_LAG] = None

    for lo in range(0, SUPER, BLK):
        sl = slice(lo, lo + BLK)
        m_all = functools.reduce(jnp.maximum, [m_ref[pi, sl, :] for pi in range(len(DILATIONS))])
        num = den = 0.0
        for pi in range(len(DILATIONS)):
            w = jnp.exp2(m_ref[pi, sl, :] - m_all)
            num = num + w * acc_ref[pi, sl, :]
            den = den + w * l_ref[pi, sl, :]
        o_ref[sl, :] = (num / den).astype(o_ref.dtype)


def _dilated_mixture(slabs_a):
    s = slabs_a.shape[1]

    blk = (1, SUPER, LANES)
    return pl.pallas_call(
        _dilated_kernel,
        grid=(N_HEADS_A, s // SUPER),
        in_specs=[
            pl.BlockSpec(blk, lambda h, sb: (h, sb, 0)),
            pl.BlockSpec(blk, lambda h, sb: (N_HEADS_A + h, sb, 0)),
            pl.BlockSpec(blk, lambda h, sb: (2 * N_HEADS_A + h, sb, 0)),
        ],
        out_specs=pl.BlockSpec((SUPER, HEAD_DIM), lambda h, sb: (sb, h)),
        out_shape=jax.ShapeDtypeStruct((s, WIDTH_A), BF16),
        scratch_shapes=([pltpu.VMEM((len(DILATIONS), SUPER, LANES), F32)] * 3
                        + [pltpu.VMEM((sum(DILATIONS), BLK, HEAD_DIM), BF16)] * 2),
        compiler_params=_params(("arbitrary", "arbitrary")),
        name="dilated",
    )(slabs_a, slabs_a, slabs_a)


def _diff_attn_kernel(lam_ref, gain_ref, qt_ref, k_ref, vt_ref, o_ref,
                      qs_ref, m_ref, acc_ref, *, t, lambda_init):
    qi = pl.program_id(1)
    kb = DIFF_KEYS
    nsub = t // kb
    qt = qt_ref[0, 0].astype(F32) * (DIFF_QK_DIM ** -0.5 * LOG2E)
    dim = lax.broadcasted_iota(jnp.int32, (HEAD_DIM, kb), 0)
    for b in range(nsub):
        blk = qt[:, b * kb:(b + 1) * kb]
        qs_ref[:, 2 * b * kb:(2 * b + 1) * kb] = jnp.where(dim < DIFF_QK_DIM, blk, 0.0).astype(BF16)
        qs_ref[:, (2 * b + 1) * kb:(2 * b + 2) * kb] = (
            jnp.where(dim >= DIFF_QK_DIM, blk, 0.0).astype(BF16))
    m_ref[...] = jnp.full(m_ref.shape, MASKED, F32)
    acc_ref[...] = jnp.zeros(acc_ref.shape, F32)
    ones = jnp.ones((DENOM_ROWS, kb), BF16)

    def chunk(c, diagonal):
        def first_col(j):
            return 2 * j * kb if diagonal else 0

        def scores(j):
            kc = k_ref[0, pl.ds(pl.multiple_of(c * t + j * kb, kb), kb), :]
            return _dot(kc, qs_ref[:, first_col(j):])

        pending = [scores(j) for j in range(min(DIFF_AHEAD, nsub))]
        for j in range(nsub):
            st = pending.pop(0)
            if j + DIFF_AHEAD < nsub:
                pending.append(scores(j + DIFF_AHEAD))
            lo = first_col(j)
            if diagonal:
                head = st[:, :2 * kb]
                key = lax.broadcasted_iota(jnp.int32, head.shape, 0)
                col = lax.broadcasted_iota(jnp.int32, head.shape, 1)
                head = jnp.where(key <= jnp.where(col >= kb, col - kb, col), head, MASKED)
                st = head if j == nsub - 1 else jnp.concatenate([head, st[:, 2 * kb:]], axis=1)
            m_old = m_ref[:, lo:]
            m_new = jnp.maximum(m_old, jnp.max(st, axis=0, keepdims=True))
            alpha = jnp.exp2(m_old - m_new)
            p = jnp.exp2(st - m_new).astype(BF16)
            vt_aug = jnp.concatenate([vt_ref[0, c, :, j * kb:(j + 1) * kb], ones], axis=0)
            acc_ref[:, lo:] = alpha * acc_ref[:, lo:] + _dot(vt_aug, p)
            m_ref[:, lo:] = m_new

    def body(c, carry):
        chunk(c, False)
        return carry

    lax.fori_loop(0, qi, body, 0)
    chunk(qi, True)

    o = acc_ref[0:HEAD_DIM, :] / acc_ref[HEAD_DIM:HEAD_DIM + 1, :]
    o1 = jnp.concatenate([o[:, 2 * b * kb:(2 * b + 1) * kb] for b in range(nsub)], axis=1)
    o2 = jnp.concatenate([o[:, (2 * b + 1) * kb:(2 * b + 2) * kb] for b in range(nsub)], axis=1)
    lam = (jnp.exp(jnp.sum(lam_ref[0:1, :] * lam_ref[1:2, :], axis=1, keepdims=True))
           - jnp.exp(jnp.sum(lam_ref[2:3, :] * lam_ref[3:4, :], axis=1, keepdims=True))
           + lambda_init)
    ob = o1 - lam * o2
    ob = ob * lax.rsqrt(jnp.mean(ob * ob, axis=0, keepdims=True) + EPS)
    o_ref[...] = (ob.T * gain_ref[...] * (1.0 - lambda_init)).astype(o_ref.dtype)


def _diff_attention(k_slabs, qv_t, lam_vecs, gain, *, t, lambda_init):
    s = k_slabs.shape[1]
    kernel = functools.partial(_diff_attn_kernel, t=t, lambda_init=lambda_init)
    return pl.pallas_call(
        kernel,
        grid=(N_HEADS_B, s // t),
        in_specs=[
            pl.BlockSpec((4, DIFF_QK_DIM), lambda h, qi: (0, 0)),
            pl.BlockSpec((1, HEAD_DIM), lambda h, qi: (0, 0)),
            pl.BlockSpec((1, 1, HEAD_DIM, t), lambda h, qi: (h, qi, 0, 0)),
            pl.BlockSpec((1, s, LANES), lambda h, qi: (h, 0, 0)),
            pl.BlockSpec((1, s // t, HEAD_DIM, t), lambda h, qi: (N_HEADS_B + h, 0, 0, 0)),
        ],
        out_specs=pl.BlockSpec((t, HEAD_DIM), lambda h, qi: (qi, h)),
        out_shape=jax.ShapeDtypeStruct((s, WIDTH_B), BF16),
        scratch_shapes=[
            pltpu.VMEM((HEAD_DIM, 2 * t), BF16),
            pltpu.VMEM((1, 2 * t), F32),
            pltpu.VMEM((HEAD_DIM + DENOM_ROWS, 2 * t), F32),
        ],
        compiler_params=_params(("arbitrary", "arbitrary")),
        name="diff_attn",
    )(lam_vecs, gain, qv_t, k_slabs, qv_t)


def _sgu_kernel(x_ref, gain_ref, wu_ref, wv_ref, lng_ref, lnb_ref, ws_ref, bs_ref, o_ref, *, tm):
    h = _rms_scale(x_ref[...], gain_ref[...]).astype(BF16)
    gelu = lambda z: 0.5 * z * (1.0 + lax.erf(z * math.sqrt(0.5)))
    u = gelu(_dot(h, wu_ref[...]))
    v = gelu(_dot(h, wv_ref[...]))
    mu = jnp.mean(v, axis=-1, keepdims=True)
    var = jnp.mean(jnp.square(v - mu), axis=-1, keepdims=True)
    vn = ((v - mu) * lax.rsqrt(var + EPS) * lng_ref[...] + lnb_ref[...]).astype(BF16)
    row = lax.broadcasted_iota(jnp.int32, (CHUNK, CHUNK), 0)
    col = lax.broadcasted_iota(jnp.int32, (CHUNK, CHUNK), 1)
    for g in range(N_HEADS_C):
        wm = jnp.where(col <= row, ws_ref[g], 0.0).astype(BF16)
        bias = bs_ref[:, g:g + 1]
        cols = slice(g * HEAD_DIM, (g + 1) * HEAD_DIM)
        for c in range(tm // CHUNK):
            rows = slice(c * CHUNK, (c + 1) * CHUNK)
            y = _dot(wm, vn[rows, cols]) + bias
            o_ref[rows, cols] = (u[rows, cols] * y).astype(o_ref.dtype)


def _spatial_gating(x, gain, w_in, ln_g, ln_b, w_s, b_s_t, *, tm):
    s = x.shape[0]
    const2 = lambda i: (0, 0)
    return pl.pallas_call(
        functools.partial(_sgu_kernel, tm=tm),
        grid=(s // tm,),
        in_specs=[
            pl.BlockSpec((tm, D_MODEL), lambda i: (i, 0)),
            pl.BlockSpec((1, D_MODEL), const2),
            pl.BlockSpec((D_MODEL, WIDTH_C), lambda i: (0, ATTN_COLS // WIDTH_C)),
            pl.BlockSpec((D_MODEL, WIDTH_C), lambda i: (0, ATTN_COLS // WIDTH_C + 1)),
            pl.BlockSpec((1, WIDTH_C), const2),
            pl.BlockSpec((1, WIDTH_C), const2),
            pl.BlockSpec((N_HEADS_C, CHUNK, CHUNK), lambda i: (0, 0, 0)),
            pl.BlockSpec((CHUNK, N_HEADS_C), const2),
        ],
        out_specs=pl.BlockSpec((tm, WIDTH_C), lambda i: (i, 0)),
        out_shape=jax.ShapeDtypeStruct((s, WIDTH_C), BF16),
        compiler_params=_params(("arbitrary",)),
        name="sgu",
    )(x, gain, w_in, w_in, ln_g, ln_b, w_s, b_s_t)


def _out_proj_kernel(x_ref, a_ref, b_ref, c_ref, wa_ref, wb_ref, wc_ref, o_ref):
    o_ref[...] = (x_ref[...] + _dot(a_ref[...], wa_ref[...]) + _dot(b_ref[...], wb_ref[...])
                  + _dot(c_ref[...], wc_ref[...]))


def _out_proj(x, a, b, c, w_out, *, tm):
    s = x.shape[0]
    return pl.pallas_call(
        _out_proj_kernel,
        grid=(s // tm,),
        in_specs=[
            pl.BlockSpec((tm, D_MODEL), lambda i: (i, 0)),
            pl.BlockSpec((tm, WIDTH_A), lambda i: (i, 0)),
            pl.BlockSpec((tm, WIDTH_B), lambda i: (i, 0)),
            pl.BlockSpec((tm, WIDTH_C), lambda i: (i, 0)),
            pl.BlockSpec((WIDTH_A, D_MODEL), lambda i: (0, 0)),
            pl.BlockSpec((WIDTH_B, D_MODEL), lambda i: (1, 0)),
            pl.BlockSpec((WIDTH_C, D_MODEL), lambda i: ((WIDTH_A + WIDTH_B) // WIDTH_C, 0)),
        ],
        out_specs=pl.BlockSpec((tm, D_MODEL), lambda i: (i, 0)),
        out_shape=jax.ShapeDtypeStruct((s, D_MODEL), F32),
        compiler_params=_params(("arbitrary",)),
        name="out_proj",
    )(x, a, b, c, w_out, w_out, w_out)


def _ffn_kernel(x_ref, gain_ref, wg_ref, wv_ref, cwg_ref, cwv_ref, cbg_ref, cbv_ref, wd_ref,
                fgain_ref, *rest, tm, final_norm, cast_next):
    i = pl.program_id(0)
    f = pl.program_id(1)
    if cast_next:
        src, rest = rest[:4], rest[4:]
        o_ref, dst, rest = rest[0], rest[1:5], rest[5:]
        for s_ref, d_ref in zip(src[:3], dst[:3]):
            d_ref[...] = s_ref[...].astype(BF16)

        @pl.when(f == 0)
        def _():
            dst[3][...] = src[3][...].astype(BF16)
    else:
        o_ref, rest = rest[0], rest[1:]
    h_ref, halo_g_ref, halo_v_ref, *u_refs = rest

    @pl.when(f == 0)
    def _():
        h_ref[...] = _rms_scale(x_ref[...], gain_ref[...]).astype(BF16)
        o_ref[...] = x_ref[...]

    @pl.when(i == 0)
    def _():
        halo_g_ref[f] = jnp.zeros(halo_g_ref.shape[1:], F32)
        halo_v_ref[f] = jnp.zeros(halo_v_ref.shape[1:], F32)

    def up(u_ref, halo_ref, w_up_ref, cols):
        u_ref[0:SUBLANES, :] = halo_ref[f, :, cols]
        u_ref[SUBLANES:SUBLANES + tm, :] = _dot(h_ref[...], w_up_ref[:, cols])
        halo_ref[f, :, cols] = u_ref[tm:tm + SUBLANES, :]

    def conv(u_ref, cw_ref, cb_ref, cols):
        out = cb_ref[:, cols]
        for tap in range(CONV_WIDTH):
            lo = SUBLANES - (CONV_WIDTH - 1) + tap
            out = out + cw_ref[tap:tap + 1, cols] * u_ref[lo:lo + tm, :]
        return out

    tf = wd_ref.shape[0]
    groups = [slice(lo, lo + FFN_GROUP) for lo in range(0, tf, FFN_GROUP)]
    def up_both(g):
        up(u_refs[2 * g], halo_g_ref, wg_ref, groups[g])
        up(u_refs[2 * g + 1], halo_v_ref, wv_ref, groups[g])

    def gated(g):
        gate = conv(u_refs[2 * g], cwg_ref, cbg_ref, groups[g])
        val = conv(u_refs[2 * g + 1], cwv_ref, cbv_ref, groups[g])
        return (gate * jax.nn.sigmoid(gate) * val).astype(BF16)

    def down(g, act):
        for lo in range(0, D_MODEL, FFN_OUT_BLOCK):
            out_cols = slice(lo, lo + FFN_OUT_BLOCK)
            o_ref[:, out_cols] += _dot(act, wd_ref[groups[g], out_cols])

    up_both(0)
    for g in range(len(groups)):
        act = gated(g)
        if g + 1 < len(groups):
            up_both(g + 1)
        down(g, act)

    if final_norm:
        @pl.when(f == pl.num_programs(1) - 1)
        def _():
            o_ref[...] = _rms_scale(o_ref[...], fgain_ref[...])


def _ffn(x, gain, w_up, conv_w, conv_b, w_down, layer, final_gain, next_weights, *, tm, tf,
         final_norm):
    s = x.shape[0]
    ni, nf = s // tm, FFN_DIM // tf
    cast_next = next_weights is not None
    kernel = functools.partial(_ffn_kernel, tm=tm, final_norm=final_norm, cast_next=cast_next)
    in_specs = [
        pl.BlockSpec((tm, D_MODEL), lambda i, f: (i, 0)),
        pl.BlockSpec((1, D_MODEL), lambda i, f: (0, 0)),
        pl.BlockSpec((D_MODEL, tf), lambda i, f: (0, f)),
        pl.BlockSpec((D_MODEL, tf), lambda i, f: (0, nf + f)),
        pl.BlockSpec((None, CONV_WIDTH, tf), lambda i, f: (layer, 0, f)),
        pl.BlockSpec((None, CONV_WIDTH, tf), lambda i, f: (layer, 0, nf + f)),
        pl.BlockSpec((None, 1, tf), lambda i, f: (layer, 0, f)),
        pl.BlockSpec((None, 1, tf), lambda i, f: (layer, 0, nf + f)),
        pl.BlockSpec((tf, D_MODEL), lambda i, f: (f, 0)),
        pl.BlockSpec((1, D_MODEL), lambda i, f: (0, 0)),
    ]
    out_specs = [pl.BlockSpec((tm, D_MODEL), lambda i, f: (i, 0))]
    out_shape = [jax.ShapeDtypeStruct((s, D_MODEL), F32)]
    operands = [x, gain, w_up, w_up, conv_w, conv_w, conv_b, conv_b, w_down, final_gain]
    if cast_next:
        def tiling(w, rows_over_f):
            rows, cols = w.shape[1:]
            if rows_over_f:
                return (rows // nf, cols // ni), lambda i, f: (f, i)
            return (rows // ni, cols // nf), lambda i, f: (i, f)

        w_in, w_up_next, w_down_next, w_out = next_weights
        for w, rows_over_f in ((w_in, False), (w_up_next, False), (w_down_next, True)):
            block, index = tiling(w, rows_over_f)
            in_specs.append(pl.BlockSpec((None,) + block,
                                         lambda i, f, index=index: (layer + 1,) + index(i, f)))
            out_specs.append(pl.BlockSpec(block, index))
            out_shape.append(jax.ShapeDtypeStruct(w.shape[1:], BF16))
        block = (w_out.shape[1] // ni, w_out.shape[2])
        in_specs.append(pl.BlockSpec((None,) + block, lambda i, f: (layer + 1, i, 0)))
        out_specs.append(pl.BlockSpec(block, lambda i, f: (i, 0)))
        out_shape.append(jax.ShapeDtypeStruct(w_out.shape[1:], BF16))
        operands += [w_in, w_up_next, w_down_next, w_out]
    return pl.pallas_call(
        kernel,
        grid=(ni, nf),
        in_specs=in_specs,
        out_specs=out_specs,
        out_shape=out_shape,
        scratch_shapes=[
            pltpu.VMEM((tm, D_MODEL), BF16),
            pltpu.VMEM((nf, SUBLANES, tf), F32),
            pltpu.VMEM((nf, SUBLANES, tf), F32),
        ] + [pltpu.VMEM((tm + SUBLANES, FFN_GROUP), F32)] * (2 * tf // FFN_GROUP),
        compiler_params=_params(("arbitrary", "arbitrary")),
        name="ffn",
    )(*operands)


def _rope_tables(seq):
    def tables(width, reps):
        rot = width // ROPE_FRACTION
        half = rot // 2
        inv = 1.0 / (ROPE_THETA ** (jnp.arange(0, rot, 2, dtype=F32) / rot))
        ang = jnp.arange(seq, dtype=F32)[:, None] * inv[None, :]
        cos, sin = jnp.cos(ang), jnp.sin(ang)
        zeros = lambda n: jnp.zeros((seq, n), F32)
        keep = jnp.concatenate([cos, cos, jnp.ones((seq, width - rot), F32)], axis=1)
        plus = jnp.concatenate([-sin, zeros(width - half)], axis=1)
        minus = jnp.concatenate([zeros(half), sin, zeros(width - rot)], axis=1)
        return jnp.stack([jnp.tile(t, (1, reps)) for t in (keep, plus, minus)])

    return jnp.stack([tables(HEAD_DIM, 1), tables(DIFF_QK_DIM, 2)])


def kernel(x, norm_mix, w_in, lambda_q1, lambda_k1, lambda_q2, lambda_k2, diff_subln,
           sgu_ln_g, sgu_ln_b, sgu_w, sgu_b, w_out, norm_ffn, w_up, conv_w, conv_b, w_down,
           norm_final):
    batch, seq, _ = x.shape
    depth = w_in.shape[0]
    tables = _rope_tables(seq)
    row = lambda v: v.reshape(1, -1)
    conv_b = conv_b[:, None, :]
    outs = []
    for bi in range(batch):
        xs = x[bi]
        wi, wu, wd, wo = (w[0].astype(BF16) for w in (w_in, w_up, w_down, w_out))
        for l in range(depth):
            last = l == depth - 1
            lambda_init = 0.8 - 0.6 * math.exp(-0.3 * l)
            slabs_a, k_slabs_b, qv_t_b = _in_proj(xs, row(norm_mix[l]), wi, tables,
                                                  tm=1024, t=DIFF_T)
            out_a = _dilated_mixture(slabs_a)
            lam_vecs = jnp.stack([lambda_q1[l], lambda_k1[l], lambda_q2[l], lambda_k2[l]])
            out_b = _diff_attention(k_slabs_b, qv_t_b, lam_vecs, row(diff_subln[l]), t=DIFF_T,
                                    lambda_init=lambda_init)
            out_c = _spatial_gating(xs, row(norm_mix[l]), wi,
                                    row(sgu_ln_g[l]), row(sgu_ln_b[l]), sgu_w[l],
                                    jnp.transpose(sgu_b[l]), tm=512)
            xs = _out_proj(xs, out_a, out_b, out_c, wo, tm=512)
            xs, *cast = _ffn(xs, row(norm_ffn[l]), wu, conv_w, conv_b, wd, l, row(norm_final),
                             None if last else (w_in, w_up, w_down, w_out),
                             tm=512, tf=512, final_norm=last)
            if not last:
                wi, wu, wd, wo = cast
        outs.append(xs)
    return jnp.stack(outs)
```

```python
import functools
import math

import jax
import jax.numpy as jnp
from jax import lax
from jax.experimental import pallas as pl
from jax.experimental.pallas import tpu as pltpu

F32 = jnp.float32
BF16 = jnp.bfloat16

D_MODEL = 2048
HEAD_DIM = 128
N_HEADS_A = 6
N_HEADS_B = 6
N_HEADS_C = 4
DIFF_QK_DIM = 64
WIDTH_A = N_HEADS_A * HEAD_DIM
WIDTH_B = N_HEADS_B * HEAD_DIM
WIDTH_C = N_HEADS_C * HEAD_DIM
ATTN_COLS = 3 * WIDTH_A + 3 * WIDTH_B
DILATIONS = (1, 4, 16)
BLK = 128
SUPER = BLK * DILATIONS[-1]
CHUNK = 128
ROPE_THETA = 500000.0
ROPE_FRACTION = 4
FFN_DIM = 5632
CONV_WIDTH = 3
EPS = 1e-6
LOG2E = math.log2(math.e)
MASKED = -1e30

LANES = 128
SUBLANES = 8
VMEM_LIMIT = 56 * 1024 * 1024
DIFF_T = 1024
DIFF_KEYS = 256
DIFF_AHEAD = 3
DENOM_ROWS = 16
IN_PROJ_BLOCK = 256
FFN_GROUP = 256
FFN_OUT_BLOCK = 512
DILATED_SOFTMAX_LAG = 2
DILATED_VALUE_LAG = 4


def _params(semantics, vmem=VMEM_LIMIT):
    return pltpu.CompilerParams(dimension_semantics=semantics, vmem_limit_bytes=vmem)


def _rms_scale(x, gain):
    return x * lax.rsqrt(jnp.mean(x * x, axis=-1, keepdims=True) + EPS) * gain


def _dot_nt(a, b):
    return lax.dot_general(a, b, (((1,), (1,)), ((), ())), preferred_element_type=F32)


def _dot(a, b):
    return jnp.dot(a, b, preferred_element_type=F32)


def _in_proj_kernel(x_ref, gain_ref, w_ref, tab_ref, oa_ref, obk_ref, obt_ref, h_ref, *, t):
    j = pl.program_id(1)

    @pl.when(j == 0)
    def _():
        h_ref[...] = _rms_scale(x_ref[...], gain_ref[...]).astype(BF16)

    def rotate(xh, half):
        return (xh * tab_ref[0, 0]
                + pltpu.roll(xh, LANES - half, 1) * tab_ref[0, 1]
                + pltpu.roll(xh, half, 1) * tab_ref[0, 2])

    rotate_a = lambda xh: rotate(xh, HEAD_DIM // ROPE_FRACTION // 2)
    rotate_b = lambda xh: rotate(xh, DIFF_QK_DIM // ROPE_FRACTION // 2)

    identity = lambda xh: xh

    def store(out_ref):
        def fn(hh, y):
            out_ref[hh] = y.astype(out_ref.dtype)
        return fn

    def store_transposed(hh, y):
        for c in range(y.shape[0] // t):
            obt_ref[hh, c] = y[c * t:(c + 1) * t, :].T.astype(obt_ref.dtype)

    def group(fn, put):
        heads_per_block = IN_PROJ_BLOCK // HEAD_DIM
        blocks = [slice(lo, lo + IN_PROJ_BLOCK) for lo in range(0, WIDTH_A, IN_PROJ_BLOCK)]
        project = lambda cols: _dot(h_ref[...], w_ref[:, cols])
        pending = project(blocks[0])
        for bi in range(len(blocks)):
            acc = pending
            if bi + 1 < len(blocks):
                pending = project(blocks[bi + 1])
            for hb in range(heads_per_block):
                put(bi * heads_per_block + hb, fn(acc[:, hb * HEAD_DIM:(hb + 1) * HEAD_DIM]))

    @pl.when(j < 2)
    def _():
        group(rotate_a, store(oa_ref))

    @pl.when(j == 2)
    def _():
        group(identity, store(oa_ref))

    @pl.when(j == 3)
    def _():
        group(rotate_b, store_transposed)

    @pl.when(j == 4)
    def _():
        group(rotate_b, store(obk_ref))

    @pl.when(j == 5)
    def _():
        group(identity, store_transposed)


def _in_proj(x, gain, w_in, tables, *, tm, t):
    s = x.shape[0]
    grid = (s // tm, 6)
    return pl.pallas_call(
        functools.partial(_in_proj_kernel, t=t),
        grid=grid,
        in_specs=[
            pl.BlockSpec((tm, D_MODEL), lambda i, j: (i, 0)),
            pl.BlockSpec((1, D_MODEL), lambda i, j: (0, 0)),
            pl.BlockSpec((D_MODEL, WIDTH_A), lambda i, j: (0, j)),
            pl.BlockSpec((1, 3, tm, LANES), lambda i, j: (j // 3, 0, i, 0)),
        ],
        out_specs=[
            pl.BlockSpec((6, tm, LANES), lambda i, j: (jnp.minimum(j, 2), i, 0)),
            pl.BlockSpec((6, tm, LANES), lambda i, j: (0, i, 0)),
            pl.BlockSpec((6, tm // t, HEAD_DIM, t), lambda i, j: (j // 5, i, 0, 0)),
        ],
        out_shape=[
            jax.ShapeDtypeStruct((18, s, LANES), F32),
            jax.ShapeDtypeStruct((6, s, LANES), BF16),
            jax.ShapeDtypeStruct((12, s // t, HEAD_DIM, t), BF16),
        ],
        scratch_shapes=[pltpu.VMEM((tm, D_MODEL), BF16)],
        compiler_params=_params(("arbitrary", "arbitrary")),
        name="in_proj",
    )(x, gain, w_in, tables)


def _dilated_kernel(q_ref, k_ref, v_ref, o_ref, m_ref, l_ref, acc_ref, kc_ref, vc_ref):
    @pl.when(pl.program_id(1) == 0)
    def _():
        kc_ref[...] = jnp.zeros(kc_ref.shape, BF16)
        vc_ref[...] = jnp.zeros(vc_ref.shape, BF16)

    row = lax.broadcasted_iota(jnp.int32, (BLK, 2 * BLK), 0)
    col = lax.broadcasted_iota(jnp.int32, (BLK, 2 * BLK), 1)
    window = jnp.where(col < BLK, col - row, row - col + BLK) >= 0
    no_prev = jnp.where(pl.program_id(1) > 0, 0, BLK)
    first_window = jnp.where(col < BLK, col - row - no_prev, row - col + BLK) >= 0
    scale = HEAD_DIM ** -0.5 * LOG2E
    ones = jnp.ones((2 * BLK, LANES), BF16)

    def rows(d, start):
        return pl.ds(start, BLK) if d == 1 else pl.ds(start, BLK, stride=d)

    chains = [(d, r) for d in DILATIONS for r in range(d)]

    def stage_scores(unit, carry):
        pi, d, r, bb = unit
        chain = chains.index((d, r))
        sl = rows(d, r + d * BLK * bb)
        q = (q_ref[0, sl, :] * scale).astype(BF16)
        k_cur = k_ref[0, sl, :].astype(BF16)
        v_cur = v_ref[0, sl, :].astype(BF16)
        if bb == 0:
            k_prev, v_prev = kc_ref[chain], vc_ref[chain]
        else:
            k_prev, v_prev = carry
        if bb == SUPER // (BLK * d) - 1:
            kc_ref[chain] = k_cur
            vc_ref[chain] = v_cur
        scores = _dot_nt(q, jnp.concatenate([k_prev, k_cur], axis=0))
        scores = jnp.where(first_window if bb == 0 else window, scores, MASKED)
        values = jnp.concatenate([jnp.concatenate([v_prev, v_cur], axis=0), ones], axis=1)
        return dict(pi=pi, sl=sl, scores=scores, values=values), (k_cur, v_cur)

    def stage_softmax(state):
        m_blk = jnp.max(state["scores"], axis=1, keepdims=True)
        state["probs"] = jnp.exp2(state.pop("scores") - m_blk).astype(BF16)
        m_ref[state["pi"], state["sl"], :] = jnp.broadcast_to(m_blk, (BLK, LANES))

    def stage_values(state):
        out = _dot(state["probs"], state["values"])
        acc_ref[state["pi"], state["sl"], :] = out[:, :HEAD_DIM]
        l_ref[state["pi"], state["sl"], :] = out[:, HEAD_DIM:]

    units = [(pi, d, r, bb) for pi, d in enumerate(DILATIONS) for r in range(d)
             for bb in range(SUPER // (BLK * d))]
    states = [None] * len(units)
    carry = None
    for i in range(len(units) + DILATED_VALUE_LAG):
        if i < len(units):
            states[i], carry = stage_scores(units[i], carry)
        if 0 <= i - DILATED_SOFTMAX_LAG < len(units):
            stage_softmax(states[i - DILATED_SOFTMAX_LAG])
        if 0 <= i - DILATED_VALUE_LAG < len(units):
            stage_values(states[i - DILATED_VALUE_LAG])
            states[i - DILATED_VALUE_LAG] = None

    for lo in range(0, SUPER, BLK):
        sl = slice(lo, lo + BLK)
        m_all = functools.reduce(jnp.maximum, [m_ref[pi, sl, :] for pi in range(len(DILATIONS))])
        num = den = 0.0
        for pi in range(len(DILATIONS)):
            w = jnp.exp2(m_ref[pi, sl, :] - m_all)
            num = num + w * acc_ref[pi, sl, :]
            den = den + w * l_ref[pi, sl, :]
        o_ref[sl, :] = (num / den).astype(o_ref.dtype)


def _dilated_mixture(slabs_a):
    s = slabs_a.shape[1]

    blk = (1, SUPER, LANES)
    return pl.pallas_call(
        _dilated_kernel,
        grid=(N_HEADS_A, s // SUPER),
        in_specs=[
            pl.BlockSpec(blk, lambda h, sb: (h, sb, 0)),
            pl.BlockSpec(blk, lambda h, sb: (N_HEADS_A + h, sb, 0)),
            pl.BlockSpec(blk, lambda h, sb: (2 * N_HEADS_A + h, sb, 0)),
        ],
        out_specs=pl.BlockSpec((SUPER, HEAD_DIM), lambda h, sb: (sb, h)),
        out_shape=jax.ShapeDtypeStruct((s, WIDTH_A), BF16),
        scratch_shapes=([pltpu.VMEM((len(DILATIONS), SUPER, LANES), F32)] * 3
                        + [pltpu.VMEM((sum(DILATIONS), BLK, HEAD_DIM), BF16)] * 2),
        compiler_params=_params(("arbitrary", "arbitrary")),
        name="dilated",
    )(slabs_a, slabs_a, slabs_a)


def _diff_attn_kernel(lam_ref, gain_ref, qt_ref, k_ref, vt_ref, *rest, t, lambda_init, cast_steps):
    qi = pl.program_id(1)
    n_cast = (len(rest) - 4) // 2
    src, rest = rest[:n_cast], rest[n_cast:]
    o_ref, dst = rest[0], rest[1:1 + n_cast]
    qs_ref, m_ref, acc_ref = rest[1 + n_cast:]
    if n_cast:
        @pl.when(pl.program_id(0) * pl.num_programs(1) + qi < cast_steps)
        def _():
            for s_ref, d_ref in zip(src, dst):
                d_ref[...] = s_ref[...].astype(BF16)

    kb = DIFF_KEYS
    nsub = t // kb
    qt = qt_ref[0, 0].astype(F32) * (DIFF_QK_DIM ** -0.5 * LOG2E)
    dim = lax.broadcasted_iota(jnp.int32, (HEAD_DIM, kb), 0)
    for b in range(nsub):
        blk = qt[:, b * kb:(b + 1) * kb]
        qs_ref[:, 2 * b * kb:(2 * b + 1) * kb] = jnp.where(dim < DIFF_QK_DIM, blk, 0.0).astype(BF16)
        qs_ref[:, (2 * b + 1) * kb:(2 * b + 2) * kb] = (
            jnp.where(dim >= DIFF_QK_DIM, blk, 0.0).astype(BF16))
    m_ref[...] = jnp.full(m_ref.shape, MASKED, F32)
    acc_ref[...] = jnp.zeros(acc_ref.shape, F32)
    ones = jnp.ones((DENOM_ROWS, kb), BF16)

    def chunk(c, diagonal):
        def first_col(j):
            return 2 * j * kb if diagonal else 0

        def scores(j):
            kc = k_ref[0, pl.ds(pl.multiple_of(c * t + j * kb, kb), kb), :]
            return _dot(kc, qs_ref[:, first_col(j):])

        pending = [scores(j) for j in range(min(DIFF_AHEAD, nsub))]
        for j in range(nsub):
            st = pending.pop(0)
            if j + DIFF_AHEAD < nsub:
                pending.append(scores(j + DIFF_AHEAD))
            lo = first_col(j)
            if diagonal:
                head = st[:, :2 * kb]
                key = lax.broadcasted_iota(jnp.int32, head.shape, 0)
                col = lax.broadcasted_iota(jnp.int32, head.shape, 1)
                head = jnp.where(key <= jnp.where(col >= kb, col - kb, col), head, MASKED)
                st = head if j == nsub - 1 else jnp.concatenate([head, st[:, 2 * kb:]], axis=1)
            m_old = m_ref[:, lo:]
            m_new = jnp.maximum(m_old, jnp.max(st, axis=0, keepdims=True))
            alpha = jnp.exp2(m_old - m_new)
            p = jnp.exp2(st - m_new).astype(BF16)
            vt_aug = jnp.concatenate([vt_ref[0, c, :, j * kb:(j + 1) * kb], ones], axis=0)
            acc_ref[:, lo:] = alpha * acc_ref[:, lo:] + _dot(vt_aug, p)
            m_ref[:, lo:] = m_new

    def body(c, carry):
        chunk(c, False)
        return carry

    lax.fori_loop(0, qi, body, 0)
    chunk(qi, True)

    o = acc_ref[0:HEAD_DIM, :] / acc_ref[HEAD_DIM:HEAD_DIM + 1, :]
    o1 = jnp.concatenate([o[:, 2 * b * kb:(2 * b + 1) * kb] for b in range(nsub)], axis=1)
    o2 = jnp.concatenate([o[:, (2 * b + 1) * kb:(2 * b + 2) * kb] for b in range(nsub)], axis=1)
    lam = (jnp.exp(jnp.sum(lam_ref[0:1, :] * lam_ref[1:2, :], axis=1, keepdims=True))
           - jnp.exp(jnp.sum(lam_ref[2:3, :] * lam_ref[3:4, :], axis=1, keepdims=True))
           + lambda_init)
    ob = o1 - lam * o2
    ob = ob * lax.rsqrt(jnp.mean(ob * ob, axis=0, keepdims=True) + EPS)
    o_ref[...] = (ob.T * gain_ref[...] * (1.0 - lambda_init)).astype(o_ref.dtype)


def _diff_attention(k_slabs, qv_t, lam_vecs, gain, cast=(), cast_layer=0, *, t, lambda_init):
    s = k_slabs.shape[1]
    nq = s // t
    cast_steps = 1
    while (2 * cast_steps <= N_HEADS_B * nq
           and all(w.shape[1] % (2 * cast_steps * 2 * SUBLANES) == 0 for w in cast)):
        cast_steps *= 2
    kernel = functools.partial(_diff_attn_kernel, t=t, lambda_init=lambda_init,
                               cast_steps=cast_steps)
    in_specs = [
        pl.BlockSpec((4, DIFF_QK_DIM), lambda h, qi: (0, 0)),
        pl.BlockSpec((1, HEAD_DIM), lambda h, qi: (0, 0)),
        pl.BlockSpec((1, 1, HEAD_DIM, t), lambda h, qi: (h, qi, 0, 0)),
        pl.BlockSpec((1, s, LANES), lambda h, qi: (h, 0, 0)),
        pl.BlockSpec((1, s // t, HEAD_DIM, t), lambda h, qi: (N_HEADS_B + h, 0, 0, 0)),
    ]
    out_specs = [pl.BlockSpec((t, HEAD_DIM), lambda h, qi: (qi, h))]
    out_shape = [jax.ShapeDtypeStruct((s, WIDTH_B), BF16)]
    block_of = lambda h, qi: jnp.minimum(h * nq + qi, cast_steps - 1)
    for w in cast:
        block = (w.shape[1] // cast_steps, w.shape[2])
        in_specs.append(pl.BlockSpec((None,) + block,
                                     lambda h, qi: (cast_layer, block_of(h, qi), 0)))
        out_specs.append(pl.BlockSpec(block, lambda h, qi: (block_of(h, qi), 0)))
        out_shape.append(jax.ShapeDtypeStruct(w.shape[1:], BF16))
    return pl.pallas_call(
        kernel,
        grid=(N_HEADS_B, nq),
        in_specs=in_specs,
        out_specs=out_specs,
        out_shape=out_shape,
        scratch_shapes=[
            pltpu.VMEM((HEAD_DIM, 2 * t), BF16),
            pltpu.VMEM((1, 2 * t), F32),
            pltpu.VMEM((HEAD_DIM + DENOM_ROWS, 2 * t), F32),
        ],
        compiler_params=_params(("arbitrary", "arbitrary")),
        name="diff_attn",
    )(lam_vecs, gain, qv_t, k_slabs, qv_t, *cast)


def _sgu_kernel(x_ref, gain_ref, wu_ref, wv_ref, lng_ref, lnb_ref, ws_ref, bs_ref, o_ref, *, tm):
    h = _rms_scale(x_ref[...], gain_ref[...]).astype(BF16)
    gelu = lambda z: 0.5 * z * (1.0 + lax.erf(z * math.sqrt(0.5)))
    u = gelu(_dot(h, wu_ref[...]))
    v = gelu(_dot(h, wv_ref[...]))
    mu = jnp.mean(v, axis=-1, keepdims=True)
    var = jnp.mean(jnp.square(v - mu), axis=-1, keepdims=True)
    vn = ((v - mu) * lax.rsqrt(var + EPS) * lng_ref[...] + lnb_ref[...]).astype(BF16)
    row = lax.broadcasted_iota(jnp.int32, (CHUNK, CHUNK), 0)
    col = lax.broadcasted_iota(jnp.int32, (CHUNK, CHUNK), 1)
    for g in range(N_HEADS_C):
        wm = jnp.where(col <= row, ws_ref[g], 0.0).astype(BF16)
        bias = bs_ref[:, g:g + 1]
        cols = slice(g * HEAD_DIM, (g + 1) * HEAD_DIM)
        for c in range(tm // CHUNK):
            rows = slice(c * CHUNK, (c + 1) * CHUNK)
            y = _dot(wm, vn[rows, cols]) + bias
            o_ref[rows, cols] = (u[rows, cols] * y).astype(o_ref.dtype)


def _spatial_gating(x, gain, w_in, ln_g, ln_b, w_s, b_s_t, *, tm):
    s = x.shape[0]
    const2 = lambda i: (0, 0)
    return pl.pallas_call(
        functools.partial(_sgu_kernel, tm=tm),
        grid=(s // tm,),
        in_specs=[
            pl.BlockSpec((tm, D_MODEL), lambda i: (i, 0)),
            pl.BlockSpec((1, D_MODEL), const2),
            pl.BlockSpec((D_MODEL, WIDTH_C), lambda i: (0, ATTN_COLS // WIDTH_C)),
            pl.BlockSpec((D_MODEL, WIDTH_C), lambda i: (0, ATTN_COLS // WIDTH_C + 1)),
            pl.BlockSpec((1, WIDTH_C), const2),
            pl.BlockSpec((1, WIDTH_C), const2),
            pl.BlockSpec((N_HEADS_C, CHUNK, CHUNK), lambda i: (0, 0, 0)),
            pl.BlockSpec((CHUNK, N_HEADS_C), const2),
        ],
        out_specs=pl.BlockSpec((tm, WIDTH_C), lambda i: (i, 0)),
        out_shape=jax.ShapeDtypeStruct((s, WIDTH_C), BF16),
        compiler_params=_params(("arbitrary",)),
        name="sgu",
    )(x, gain, w_in, w_in, ln_g, ln_b, w_s, b_s_t)


def _out_proj_kernel(x_ref, a_ref, b_ref, c_ref, wa_ref, wb_ref, wc_ref, o_ref):
    o_ref[...] = (x_ref[...] + _dot(a_ref[...], wa_ref[...]) + _dot(b_ref[...], wb_ref[...])
                  + _dot(c_ref[...], wc_ref[...]))


def _out_proj(x, a, b, c, w_out, *, tm):
    s = x.shape[0]
    return pl.pallas_call(
        _out_proj_kernel,
        grid=(s // tm,),
        in_specs=[
            pl.BlockSpec((tm, D_MODEL), lambda i: (i, 0)),
            pl.BlockSpec((tm, WIDTH_A), lambda i: (i, 0)),
            pl.BlockSpec((tm, WIDTH_B), lambda i: (i, 0)),
            pl.BlockSpec((tm, WIDTH_C), lambda i: (i, 0)),
            pl.BlockSpec((WIDTH_A, D_MODEL), lambda i: (0, 0)),
            pl.BlockSpec((WIDTH_B, D_MODEL), lambda i: (1, 0)),
            pl.BlockSpec((WIDTH_C, D_MODEL), lambda i: ((WIDTH_A + WIDTH_B) // WIDTH_C, 0)),
        ],
        out_specs=pl.BlockSpec((tm, D_MODEL), lambda i: (i, 0)),
        out_shape=jax.ShapeDtypeStruct((s, D_MODEL), F32),
        compiler_params=_params(("arbitrary",)),
        name="out_proj",
    )(x, a, b, c, w_out, w_out, w_out)


def _ffn_kernel(x_ref, gain_ref, wg_ref, wv_ref, cwg_ref, cwv_ref, cbg_ref, cbv_ref, wd_ref,
                fgain_ref, *rest, tm, final_norm, cast_next):
    i = pl.program_id(0)
    f = pl.program_id(1)
    if cast_next:
        src, rest = rest[:4], rest[4:]
        o_ref, dst, rest = rest[0], rest[1:5], rest[5:]
        for s_ref, d_ref in zip(src[:3], dst[:3]):
            d_ref[...] = s_ref[...].astype(BF16)

        @pl.when(f == 0)
        def _():
            dst[3][...] = src[3][...].astype(BF16)
    else:
        o_ref, rest = rest[0], rest[1:]
    h_ref, halo_g_ref, halo_v_ref, *u_refs = rest

    @pl.when(f == 0)
    def _():
        h_ref[...] = _rms_scale(x_ref[...], gain_ref[...]).astype(BF16)
        o_ref[...] = x_ref[...]

    @pl.when(i == 0)
    def _():
        halo_g_ref[f] = jnp.zeros(halo_g_ref.shape[1:], F32)
        halo_v_ref[f] = jnp.zeros(halo_v_ref.shape[1:], F32)

    def up(u_ref, halo_ref, w_up_ref, cols):
        u_ref[0:SUBLANES, :] = halo_ref[f, :, cols]
        u_ref[SUBLANES:SUBLANES + tm, :] = _dot(h_ref[...], w_up_ref[:, cols])
        halo_ref[f, :, cols] = u_ref[tm:tm + SUBLANES, :]

    def conv(u_ref, cw_ref, cb_ref, cols):
        out = cb_ref[:, cols]
        for tap in range(CONV_WIDTH):
            lo = SUBLANES - (CONV_WIDTH - 1) + tap
            out = out + cw_ref[tap:tap + 1, cols] * u_ref[lo:lo + tm, :]
        return out

    tf = wd_ref.shape[0]
    groups = [slice(lo, lo + FFN_GROUP) for lo in range(0, tf, FFN_GROUP)]
    def up_both(g):
        up(u_refs[2 * g], halo_g_ref, wg_ref, groups[g])
        up(u_refs[2 * g + 1], halo_v_ref, wv_ref, groups[g])

    def gated(g):
        gate = conv(u_refs[2 * g], cwg_ref, cbg_ref, groups[g])
        val = conv(u_refs[2 * g + 1], cwv_ref, cbv_ref, groups[g])
        return (gate * jax.nn.sigmoid(gate) * val).astype(BF16)

    def down(g, act):
        for lo in range(0, D_MODEL, FFN_OUT_BLOCK):
            out_cols = slice(lo, lo + FFN_OUT_BLOCK)
            o_ref[:, out_cols] += _dot(act, wd_ref[groups[g], out_cols])

    up_both(0)
    for g in range(len(groups)):
        act = gated(g)
        if g + 1 < len(groups):
            up_both(g + 1)
        down(g, act)

    if final_norm:
        @pl.when(f == pl.num_programs(1) - 1)
        def _():
            o_ref[...] = _rms_scale(o_ref[...], fgain_ref[...])


def _ffn(x, gain, w_up, conv_w, conv_b, w_down, layer, final_gain, next_weights, *, tm, tf,
         final_norm):
    s = x.shape[0]
    ni, nf = s // tm, FFN_DIM // tf
    cast_next = next_weights is not None
    kernel = functools.partial(_ffn_kernel, tm=tm, final_norm=final_norm, cast_next=cast_next)
    in_specs = [
        pl.BlockSpec((tm, D_MODEL), lambda i, f: (i, 0)),
        pl.BlockSpec((1, D_MODEL), lambda i, f: (0, 0)),
        pl.BlockSpec((D_MODEL, tf), lambda i, f: (0, f)),
        pl.BlockSpec((D_MODEL, tf), lambda i, f: (0, nf + f)),
        pl.BlockSpec((None, CONV_WIDTH, tf), lambda i, f: (layer, 0, f)),
        pl.BlockSpec((None, CONV_WIDTH, tf), lambda i, f: (layer, 0, nf + f)),
        pl.BlockSpec((None, 1, tf), lambda i, f: (layer, 0, f)),
        pl.BlockSpec((None, 1, tf), lambda i, f: (layer, 0, nf + f)),
        pl.BlockSpec((tf, D_MODEL), lambda i, f: (f, 0)),
        pl.BlockSpec((1, D_MODEL), lambda i, f: (0, 0)),
    ]
    out_specs = [pl.BlockSpec((tm, D_MODEL), lambda i, f: (i, 0))]
    out_shape = [jax.ShapeDtypeStruct((s, D_MODEL), F32)]
    operands = [x, gain, w_up, w_up, conv_w, conv_w, conv_b, conv_b, w_down, final_gain]
    if cast_next:
        def tiling(w, rows_over_f):
            rows, cols = w.shape[1:]
            nr, nc = (nf, ni) if rows_over_f else (ni, nf)
            assert rows % nr == 0 and cols % nc == 0, (w.shape, ni, nf)
            if rows_over_f:
                return (rows // nr, cols // nc), lambda i, f: (f, i)
            return (rows // nr, cols // nc), lambda i, f: (i, f)

        w_in, w_up_next, w_down_next, w_out = next_weights
        for w, rows_over_f in ((w_in, False), (w_up_next, False), (w_down_next, True)):
            block, index = tiling(w, rows_over_f)
            in_specs.append(pl.BlockSpec((None,) + block,
                                         lambda i, f, index=index: (layer + 1,) + index(i, f)))
            out_specs.append(pl.BlockSpec(block, index))
            out_shape.append(jax.ShapeDtypeStruct(w.shape[1:], BF16))
        block = (w_out.shape[1] // ni, w_out.shape[2])
        in_specs.append(pl.BlockSpec((None,) + block, lambda i, f: (layer + 1, i, 0)))
        out_specs.append(pl.BlockSpec(block, lambda i, f: (i, 0)))
        out_shape.append(jax.ShapeDtypeStruct(w_out.shape[1:], BF16))
        operands += [w_in, w_up_next, w_down_next, w_out]
    return pl.pallas_call(
        kernel,
        grid=(ni, nf),
        in_specs=in_specs,
        out_specs=out_specs,
        out_shape=out_shape,
        scratch_shapes=[
            pltpu.VMEM((tm, D_MODEL), BF16),
            pltpu.VMEM((nf, SUBLANES, tf), F32),
            pltpu.VMEM((nf, SUBLANES, tf), F32),
        ] + [pltpu.VMEM((tm + SUBLANES, FFN_GROUP), F32)] * (2 * tf // FFN_GROUP),
        compiler_params=_params(("arbitrary", "arbitrary")),
        name="ffn",
    )(*operands)


def _rope_tables(seq):
    def tables(width, reps):
        rot = width // ROPE_FRACTION
        half = rot // 2
        inv = 1.0 / (ROPE_THETA ** (jnp.arange(0, rot, 2, dtype=F32) / rot))
        ang = jnp.arange(seq, dtype=F32)[:, None] * inv[None, :]
        cos, sin = jnp.cos(ang), jnp.sin(ang)
        zeros = lambda n: jnp.zeros((seq, n), F32)
        keep = jnp.concatenate([cos, cos, jnp.ones((seq, width - rot), F32)], axis=1)
        plus = jnp.concatenate([-sin, zeros(width - half)], axis=1)
        minus = jnp.concatenate([zeros(half), sin, zeros(width - rot)], axis=1)
        return jnp.stack([jnp.tile(t, (1, reps)) for t in (keep, plus, minus)])

    return jnp.stack([tables(HEAD_DIM, 1), tables(DIFF_QK_DIM, 2)])


def kernel(x, norm_mix, w_in, lambda_q1, lambda_k1, lambda_q2, lambda_k2, diff_subln,
           sgu_ln_g, sgu_ln_b, sgu_w, sgu_b, w_out, norm_ffn, w_up, conv_w, conv_b, w_down,
           norm_final):
    batch, seq, _ = x.shape
    depth = w_in.shape[0]
    tables = _rope_tables(seq)
    row = lambda v: v.reshape(1, -1)
    conv_b = conv_b[:, None, :]
    outs = []
    for bi in range(batch):
        xs = x[bi]
        wi = w_in[0].astype(BF16)
        for l in range(depth):
            last = l == depth - 1
            lambda_init = 0.8 - 0.6 * math.exp(-0.3 * l)
            slabs_a, k_slabs_b, qv_t_b = _in_proj(xs, row(norm_mix[l]), wi, tables,
                                                  tm=1024, t=DIFF_T)
            out_a = _dilated_mixture(slabs_a)
            lam_vecs = jnp.stack([lambda_q1[l], lambda_k1[l], lambda_q2[l], lambda_k2[l]])
            out_b, *cast = _diff_attention(k_slabs_b, qv_t_b, lam_vecs, row(diff_subln[l]),
                                           (w_up, w_down, w_out) if l == 0 else (),
                                           t=DIFF_T, lambda_init=lambda_init)
            if l == 0:
                wu, wd, wo = cast
            out_c = _spatial_gating(xs, row(norm_mix[l]), wi,
                                    row(sgu_ln_g[l]), row(sgu_ln_b[l]), sgu_w[l],
                                    jnp.transpose(sgu_b[l]), tm=512)
            xs = _out_proj(xs, out_a, out_b, out_c, wo, tm=512)
            xs, *cast = _ffn(xs, row(norm_ffn[l]), wu, conv_w, conv_b, wd, l, row(norm_final),
                             None if last else (w_in, w_up, w_down, w_out),
                             tm=512, tf=512, final_norm=last)
            if not last:
                wi, wu, wd, wo = cast
        outs.append(xs)
    return jnp.stack(outs)
```

```python
import functools
import math

import jax
import jax.numpy as jnp
from jax import lax
from jax.experimental import pallas as pl
from jax.experimental.pallas import tpu as pltpu

F32 = jnp.float32
BF16 = jnp.bfloat16

D_MODEL = 2048
HEAD_DIM = 128
N_HEADS_A = 6
N_HEADS_B = 6
N_HEADS_C = 4
DIFF_QK_DIM = 64
WIDTH_A = N_HEADS_A * HEAD_DIM
WIDTH_B = N_HEADS_B * HEAD_DIM
WIDTH_C = N_HEADS_C * HEAD_DIM
ATTN_COLS = 3 * WIDTH_A + 3 * WIDTH_B
DILATIONS = (1, 4, 16)
BLK = 128
SUPER = BLK * DILATIONS[-1]
CHUNK = 128
ROPE_THETA = 500000.0
ROPE_FRACTION = 4
FFN_DIM = 5632
CONV_WIDTH = 3
EPS = 1e-6
LOG2E = math.log2(math.e)
MASKED = -1e30

LANES = 128
SUBLANES = 8
VMEM_LIMIT = 56 * 1024 * 1024
DIFF_T = 1024
DIFF_KEYS = 256
DIFF_AHEAD = 3
DENOM_ROWS = 16
IN_PROJ_BLOCK = 256
FFN_GROUP = 256
FFN_OUT_BLOCK = 512
DILATED_SOFTMAX_LAG = 2
DILATED_VALUE_LAG = 4


def _params(semantics, vmem=VMEM_LIMIT):
    return pltpu.CompilerParams(dimension_semantics=semantics, vmem_limit_bytes=vmem)


def _rms_scale(x, gain):
    return x * lax.rsqrt(jnp.mean(x * x, axis=-1, keepdims=True) + EPS) * gain


def _dot_nt(a, b):
    return lax.dot_general(a, b, (((1,), (1,)), ((), ())), preferred_element_type=F32)


def _dot(a, b):
    return jnp.dot(a, b, preferred_element_type=F32)


def _in_proj_kernel(h_ref, w_ref, tab_ref, oa_ref, obk_ref, obt_ref, *, t):
    j = pl.program_id(1)

    def rotate(xh, half):
        return (xh * tab_ref[0, 0]
                + pltpu.roll(xh, LANES - half, 1) * tab_ref[0, 1]
                + pltpu.roll(xh, half, 1) * tab_ref[0, 2])

    rotate_a = lambda xh: rotate(xh, HEAD_DIM // ROPE_FRACTION // 2)
    rotate_b = lambda xh: rotate(xh, DIFF_QK_DIM // ROPE_FRACTION // 2)

    identity = lambda xh: xh

    def store(out_ref):
        def fn(hh, y):
            out_ref[hh] = y.astype(out_ref.dtype)
        return fn

    def store_transposed(hh, y):
        for c in range(y.shape[0] // t):
            obt_ref[hh, c] = y[c * t:(c + 1) * t, :].T.astype(obt_ref.dtype)

    def group(fn, put):
        heads_per_block = IN_PROJ_BLOCK // HEAD_DIM
        blocks = [slice(lo, lo + IN_PROJ_BLOCK) for lo in range(0, WIDTH_A, IN_PROJ_BLOCK)]
        project = lambda cols: _dot(h_ref[...], w_ref[:, cols])
        pending = project(blocks[0])
        for bi in range(len(blocks)):
            acc = pending
            if bi + 1 < len(blocks):
                pending = project(blocks[bi + 1])
            for hb in range(heads_per_block):
                put(bi * heads_per_block + hb, fn(acc[:, hb * HEAD_DIM:(hb + 1) * HEAD_DIM]))

    @pl.when(j < 2)
    def _():
        group(rotate_a, store(oa_ref))

    @pl.when(j == 2)
    def _():
        group(identity, store(oa_ref))

    @pl.when(j == 3)
    def _():
        group(rotate_b, store_transposed)

    @pl.when(j == 4)
    def _():
        group(rotate_b, store(obk_ref))

    @pl.when(j == 5)
    def _():
        group(identity, store_transposed)


def _in_proj(h, w_in, tables, *, tm, t):
    s = h.shape[0]
    grid = (s // tm, 6)
    return pl.pallas_call(
        functools.partial(_in_proj_kernel, t=t),
        grid=grid,
        in_specs=[
            pl.BlockSpec((tm, D_MODEL), lambda i, j: (i, 0)),
            pl.BlockSpec((D_MODEL, WIDTH_A), lambda i, j: (0, j)),
            pl.BlockSpec((1, 3, tm, LANES), lambda i, j: (j // 3, 0, i, 0)),
        ],
        out_specs=[
            pl.BlockSpec((6, tm, LANES), lambda i, j: (jnp.minimum(j, 2), i, 0)),
            pl.BlockSpec((6, tm, LANES), lambda i, j: (0, i, 0)),
            pl.BlockSpec((6, tm // t, HEAD_DIM, t), lambda i, j: (j // 5, i, 0, 0)),
        ],
        out_shape=[
            jax.ShapeDtypeStruct((18, s, LANES), F32),
            jax.ShapeDtypeStruct((6, s, LANES), BF16),
            jax.ShapeDtypeStruct((12, s // t, HEAD_DIM, t), BF16),
        ],
        compiler_params=_params(("arbitrary", "arbitrary")),
        name="in_proj",
    )(h, w_in, tables)


def _dilated_kernel(q_ref, k_ref, v_ref, o_ref, m_ref, l_ref, acc_ref, kc_ref, vc_ref):
    @pl.when(pl.program_id(1) == 0)
    def _():
        kc_ref[...] = jnp.zeros(kc_ref.shape, BF16)
        vc_ref[...] = jnp.zeros(vc_ref.shape, BF16)

    row = lax.broadcasted_iota(jnp.int32, (BLK, 2 * BLK), 0)
    col = lax.broadcasted_iota(jnp.int32, (BLK, 2 * BLK), 1)
    window = jnp.where(col < BLK, col - row, row - col + BLK) >= 0
    no_prev = jnp.where(pl.program_id(1) > 0, 0, BLK)
    first_window = jnp.where(col < BLK, col - row - no_prev, row - col + BLK) >= 0
    scale = HEAD_DIM ** -0.5 * LOG2E
    ones = jnp.ones((2 * BLK, LANES), BF16)

    def rows(d, start):
        return pl.ds(start, BLK) if d == 1 else pl.ds(start, BLK, stride=d)

    chains = [(d, r) for d in DILATIONS for r in range(d)]

    def stage_scores(unit, carry):
        pi, d, r, bb = unit
        chain = chains.index((d, r))
        sl = rows(d, r + d * BLK * bb)
        q = (q_ref[0, sl, :] * scale).astype(BF16)
        k_cur = k_ref[0, sl, :].astype(BF16)
        v_cur = v_ref[0, sl, :].astype(BF16)
        if bb == 0:
            k_prev, v_prev = kc_ref[chain], vc_ref[chain]
        else:
            k_prev, v_prev = carry
        if bb == SUPER // (BLK * d) - 1:
            kc_ref[chain] = k_cur
            vc_ref[chain] = v_cur
        scores = _dot_nt(q, jnp.concatenate([k_prev, k_cur], axis=0))
        scores = jnp.where(first_window if bb == 0 else window, scores, MASKED)
        values = jnp.concatenate([jnp.concatenate([v_prev, v_cur], axis=0), ones], axis=1)
        return dict(pi=pi, sl=sl, scores=scores, values=values), (k_cur, v_cur)

    def stage_softmax(state):
        m_blk = jnp.max(state["scores"], axis=1, keepdims=True)
        state["probs"] = jnp.exp2(state.pop("scores") - m_blk).astype(BF16)
        m_ref[state["pi"], state["sl"], :] = jnp.broadcast_to(m_blk, (BLK, LANES))

    def stage_values(state):
        out = _dot(state["probs"], state["values"])
        acc_ref[state["pi"], state["sl"], :] = out[:, :HEAD_DIM]
        l_ref[state["pi"], state["sl"], :] = out[:, HEAD_DIM:]

    units = [(pi, d, r, bb) for pi, d in enumerate(DILATIONS) for r in range(d)
             for bb in range(SUPER // (BLK * d))]
    states = [None] * len(units)
    carry = None
    for i in range(len(units) + DILATED_VALUE_LAG):
        if i < len(units):
            states[i], carry = stage_scores(units[i], carry)
        if 0 <= i - DILATED_SOFTMAX_LAG < len(units):
            stage_softmax(states[i - DILATED_SOFTMAX_LAG])
        if 0 <= i - DILATED_VALUE_LAG < len(units):
            stage_values(states[i - DILATED_VALUE_LAG])
            states[i - DILATED_VALUE_LAG] = None

    for lo in range(0, SUPER, BLK):
        sl = slice(lo, lo + BLK)
        m_all = functools.reduce(jnp.maximum, [m_ref[pi, sl, :] for pi in range(len(DILATIONS))])
        num = den = 0.0
        for pi in range(len(DILATIONS)):
            w = jnp.exp2(m_ref[pi, sl, :] - m_all)
            num = num + w * acc_ref[pi, sl, :]
            den = den + w * l_ref[pi, sl, :]
        o_ref[sl, :] = (num / den).astype(o_ref.dtype)


def _dilated_mixture(slabs_a):
    s = slabs_a.shape[1]

    blk = (1, SUPER, LANES)
    return pl.pallas_call(
        _dilated_kernel,
        grid=(N_HEADS_A, s // SUPER),
        in_specs=[
            pl.BlockSpec(blk, lambda h, sb: (h, sb, 0)),
            pl.BlockSpec(blk, lambda h, sb: (N_HEADS_A + h, sb, 0)),
            pl.BlockSpec(blk, lambda h, sb: (2 * N_HEADS_A + h, sb, 0)),
        ],
        out_specs=pl.BlockSpec((SUPER, HEAD_DIM), lambda h, sb: (sb, h)),
        out_shape=jax.ShapeDtypeStruct((s, WIDTH_A), BF16),
        scratch_shapes=([pltpu.VMEM((len(DILATIONS), SUPER, LANES), F32)] * 3
                        + [pltpu.VMEM((sum(DILATIONS), BLK, HEAD_DIM), BF16)] * 2),
        compiler_params=_params(("arbitrary", "arbitrary")),
        name="dilated",
    )(slabs_a, slabs_a, slabs_a)


def _diff_attn_kernel(lam_ref, gain_ref, qt_ref, k_ref, vt_ref, *rest, t, lambda_init, cast_steps):
    qi = pl.program_id(1)
    n_cast = (len(rest) - 4) // 2
    src, rest = rest[:n_cast], rest[n_cast:]
    o_ref, dst = rest[0], rest[1:1 + n_cast]
    qs_ref, m_ref, acc_ref = rest[1 + n_cast:]
    if n_cast:
        @pl.when(pl.program_id(0) * pl.num_programs(1) + qi < cast_steps)
        def _():
            for s_ref, d_ref in zip(src, dst):
                d_ref[...] = s_ref[...].astype(BF16)

    kb = DIFF_KEYS
    nsub = t // kb
    qt = qt_ref[0, 0].astype(F32) * (DIFF_QK_DIM ** -0.5 * LOG2E)
    dim = lax.broadcasted_iota(jnp.int32, (HEAD_DIM, kb), 0)
    for b in range(nsub):
        blk = qt[:, b * kb:(b + 1) * kb]
        qs_ref[:, 2 * b * kb:(2 * b + 1) * kb] = jnp.where(dim < DIFF_QK_DIM, blk, 0.0).astype(BF16)
        qs_ref[:, (2 * b + 1) * kb:(2 * b + 2) * kb] = (
            jnp.where(dim >= DIFF_QK_DIM, blk, 0.0).astype(BF16))
    m_ref[...] = jnp.full(m_ref.shape, MASKED, F32)
    acc_ref[...] = jnp.zeros(acc_ref.shape, F32)
    ones = jnp.ones((DENOM_ROWS, kb), BF16)

    def chunk(c, diagonal):
        def first_col(j):
            return 2 * j * kb if diagonal else 0

        def scores(j):
            kc = k_ref[0, pl.ds(pl.multiple_of(c * t + j * kb, kb), kb), :]
            return _dot(kc, qs_ref[:, first_col(j):])

        pending = [scores(j) for j in range(min(DIFF_AHEAD, nsub))]
        for j in range(nsub):
            st = pending.pop(0)
            if j + DIFF_AHEAD < nsub:
                pending.append(scores(j + DIFF_AHEAD))
            lo = first_col(j)
            if diagonal:
                head = st[:, :2 * kb]
                key = lax.broadcasted_iota(jnp.int32, head.shape, 0)
                col = lax.broadcasted_iota(jnp.int32, head.shape, 1)
                head = jnp.where(key <= jnp.where(col >= kb, col - kb, col), head, MASKED)
                st = head if j == nsub - 1 else jnp.concatenate([head, st[:, 2 * kb:]], axis=1)
            m_old = m_ref[:, lo:]
            m_new = jnp.maximum(m_old, jnp.max(st, axis=0, keepdims=True))
            alpha = jnp.exp2(m_old - m_new)
            p = jnp.exp2(st - m_new).astype(BF16)
            vt_aug = jnp.concatenate([vt_ref[0, c, :, j * kb:(j + 1) * kb], ones], axis=0)
            acc_ref[:, lo:] = alpha * acc_ref[:, lo:] + _dot(vt_aug, p)
            m_ref[:, lo:] = m_new

    def body(c, carry):
        chunk(c, False)
        return carry

    lax.fori_loop(0, qi, body, 0)
    chunk(qi, True)

    o = acc_ref[0:HEAD_DIM, :] / acc_ref[HEAD_DIM:HEAD_DIM + 1, :]
    o1 = jnp.concatenate([o[:, 2 * b * kb:(2 * b + 1) * kb] for b in range(nsub)], axis=1)
    o2 = jnp.concatenate([o[:, (2 * b + 1) * kb:(2 * b + 2) * kb] for b in range(nsub)], axis=1)
    lam = (jnp.exp(jnp.sum(lam_ref[0:1, :] * lam_ref[1:2, :], axis=1, keepdims=True))
           - jnp.exp(jnp.sum(lam_ref[2:3, :] * lam_ref[3:4, :], axis=1, keepdims=True))
           + lambda_init)
    ob = o1 - lam * o2
    ob = ob * lax.rsqrt(jnp.mean(ob * ob, axis=0, keepdims=True) + EPS)
    o_ref[...] = (ob.T * gain_ref[...] * (1.0 - lambda_init)).astype(o_ref.dtype)


def _diff_attention(k_slabs, qv_t, lam_vecs, gain, cast=(), cast_layer=0, *, t, lambda_init):
    s = k_slabs.shape[1]
    nq = s // t
    cast_steps = 1
    while (2 * cast_steps <= N_HEADS_B * nq
           and all(w.shape[1] % (2 * cast_steps * 2 * SUBLANES) == 0 for w in cast)):
        cast_steps *= 2
    kernel = functools.partial(_diff_attn_kernel, t=t, lambda_init=lambda_init,
                               cast_steps=cast_steps)
    in_specs = [
        pl.BlockSpec((4, DIFF_QK_DIM), lambda h, qi: (0, 0)),
        pl.BlockSpec((1, HEAD_DIM), lambda h, qi: (0, 0)),
        pl.BlockSpec((1, 1, HEAD_DIM, t), lambda h, qi: (h, qi, 0, 0)),
        pl.BlockSpec((1, s, LANES), lambda h, qi: (h, 0, 0)),
        pl.BlockSpec((1, s // t, HEAD_DIM, t), lambda h, qi: (N_HEADS_B + h, 0, 0, 0)),
    ]
    out_specs = [pl.BlockSpec((t, HEAD_DIM), lambda h, qi: (qi, h))]
    out_shape = [jax.ShapeDtypeStruct((s, WIDTH_B), BF16)]
    block_of = lambda h, qi: jnp.minimum(h * nq + qi, cast_steps - 1)
    for w in cast:
        block = (w.shape[1] // cast_steps, w.shape[2])
        in_specs.append(pl.BlockSpec((None,) + block,
                                     lambda h, qi: (cast_layer, block_of(h, qi), 0)))
        out_specs.append(pl.BlockSpec(block, lambda h, qi: (block_of(h, qi), 0)))
        out_shape.append(jax.ShapeDtypeStruct(w.shape[1:], BF16))
    return pl.pallas_call(
        kernel,
        grid=(N_HEADS_B, nq),
        in_specs=in_specs,
        out_specs=out_specs,
        out_shape=out_shape,
        scratch_shapes=[
            pltpu.VMEM((HEAD_DIM, 2 * t), BF16),
            pltpu.VMEM((1, 2 * t), F32),
            pltpu.VMEM((HEAD_DIM + DENOM_ROWS, 2 * t), F32),
        ],
        compiler_params=_params(("arbitrary", "arbitrary")),
        name="diff_attn",
    )(lam_vecs, gain, qv_t, k_slabs, qv_t, *cast)


def _sgu_kernel(x_ref, gain_ref, wu_ref, wv_ref, lng_ref, lnb_ref, ws_ref, bs_ref, o_ref, h_ref,
                *, tm):
    h = _rms_scale(x_ref[...], gain_ref[...]).astype(BF16)
    h_ref[...] = h
    gelu = lambda z: 0.5 * z * (1.0 + lax.erf(z * math.sqrt(0.5)))
    u = gelu(_dot(h, wu_ref[...]))
    v = gelu(_dot(h, wv_ref[...]))
    mu = jnp.mean(v, axis=-1, keepdims=True)
    var = jnp.mean(jnp.square(v - mu), axis=-1, keepdims=True)
    vn = ((v - mu) * lax.rsqrt(var + EPS) * lng_ref[...] + lnb_ref[...]).astype(BF16)
    row = lax.broadcasted_iota(jnp.int32, (CHUNK, CHUNK), 0)
    col = lax.broadcasted_iota(jnp.int32, (CHUNK, CHUNK), 1)
    for g in range(N_HEADS_C):
        wm = jnp.where(col <= row, ws_ref[g], 0.0).astype(BF16)
        bias = bs_ref[:, g:g + 1]
        cols = slice(g * HEAD_DIM, (g + 1) * HEAD_DIM)
        for c in range(tm // CHUNK):
            rows = slice(c * CHUNK, (c + 1) * CHUNK)
            y = _dot(wm, vn[rows, cols]) + bias
            o_ref[rows, cols] = (u[rows, cols] * y).astype(o_ref.dtype)


def _spatial_gating(x, gain, w_in, ln_g, ln_b, w_s, b_s_t, *, tm):
    s = x.shape[0]
    const2 = lambda i: (0, 0)
    return pl.pallas_call(
        functools.partial(_sgu_kernel, tm=tm),
        grid=(s // tm,),
        in_specs=[
            pl.BlockSpec((tm, D_MODEL), lambda i: (i, 0)),
            pl.BlockSpec((1, D_MODEL), const2),
            pl.BlockSpec((D_MODEL, WIDTH_C), lambda i: (0, ATTN_COLS // WIDTH_C)),
            pl.BlockSpec((D_MODEL, WIDTH_C), lambda i: (0, ATTN_COLS // WIDTH_C + 1)),
            pl.BlockSpec((1, WIDTH_C), const2),
            pl.BlockSpec((1, WIDTH_C), const2),
            pl.BlockSpec((N_HEADS_C, CHUNK, CHUNK), lambda i: (0, 0, 0)),
            pl.BlockSpec((CHUNK, N_HEADS_C), const2),
        ],
        out_specs=[pl.BlockSpec((tm, WIDTH_C), lambda i: (i, 0)),
                   pl.BlockSpec((tm, D_MODEL), lambda i: (i, 0))],
        out_shape=[jax.ShapeDtypeStruct((s, WIDTH_C), BF16),
                   jax.ShapeDtypeStruct((s, D_MODEL), BF16)],
        compiler_params=_params(("arbitrary",)),
        name="sgu",
    )(x, gain, w_in, w_in, ln_g, ln_b, w_s, b_s_t)


def _out_proj_kernel(x_ref, a_ref, b_ref, c_ref, wa_ref, wb_ref, wc_ref, o_ref):
    o_ref[...] = (x_ref[...] + _dot(a_ref[...], wa_ref[...]) + _dot(b_ref[...], wb_ref[...])
                  + _dot(c_ref[...], wc_ref[...]))


def _out_proj(x, a, b, c, w_out, *, tm):
    s = x.shape[0]
    return pl.pallas_call(
        _out_proj_kernel,
        grid=(s // tm,),
        in_specs=[
            pl.BlockSpec((tm, D_MODEL), lambda i: (i, 0)),
            pl.BlockSpec((tm, WIDTH_A), lambda i: (i, 0)),
            pl.BlockSpec((tm, WIDTH_B), lambda i: (i, 0)),
            pl.BlockSpec((tm, WIDTH_C), lambda i: (i, 0)),
            pl.BlockSpec((WIDTH_A, D_MODEL), lambda i: (0, 0)),
            pl.BlockSpec((WIDTH_B, D_MODEL), lambda i: (1, 0)),
            pl.BlockSpec((WIDTH_C, D_MODEL), lambda i: ((WIDTH_A + WIDTH_B) // WIDTH_C, 0)),
        ],
        out_specs=pl.BlockSpec((tm, D_MODEL), lambda i: (i, 0)),
        out_shape=jax.ShapeDtypeStruct((s, D_MODEL), F32),
        compiler_params=_params(("arbitrary",)),
        name="out_proj",
    )(x, a, b, c, w_out, w_out, w_out)


def _ffn_kernel(x_ref, gain_ref, wg_ref, wv_ref, cwg_ref, cwv_ref, cbg_ref, cbv_ref, wd_ref,
                fgain_ref, *rest, tm, final_norm, cast_next):
    i = pl.program_id(0)
    f = pl.program_id(1)
    if cast_next:
        src, rest = rest[:4], rest[4:]
        o_ref, dst, rest = rest[0], rest[1:5], rest[5:]

        @pl.when(f == 0)
        def _():
            dst[3][...] = src[3][...].astype(BF16)
    else:
        o_ref, rest = rest[0], rest[1:]
    h_ref, halo_g_ref, halo_v_ref, *u_refs = rest

    @pl.when(f == 0)
    def _():
        h_ref[...] = _rms_scale(x_ref[...], gain_ref[...]).astype(BF16)
        o_ref[...] = x_ref[...]

    @pl.when(i == 0)
    def _():
        halo_g_ref[f] = jnp.zeros(halo_g_ref.shape[1:], F32)
        halo_v_ref[f] = jnp.zeros(halo_v_ref.shape[1:], F32)

    def up(u_ref, halo_ref, w_up_ref, cols):
        u_ref[0:SUBLANES, :] = halo_ref[f, :, cols]
        u_ref[SUBLANES:SUBLANES + tm, :] = _dot(h_ref[...], w_up_ref[:, cols])
        halo_ref[f, :, cols] = u_ref[tm:tm + SUBLANES, :]

    def conv(u_ref, cw_ref, cb_ref, cols):
        out = cb_ref[:, cols]
        for tap in range(CONV_WIDTH):
            lo = SUBLANES - (CONV_WIDTH - 1) + tap
            out = out + cw_ref[tap:tap + 1, cols] * u_ref[lo:lo + tm, :]
        return out

    tf = wd_ref.shape[0]
    groups = [slice(lo, lo + FFN_GROUP) for lo in range(0, tf, FFN_GROUP)]
    def up_both(g):
        up(u_refs[2 * g], halo_g_ref, wg_ref, groups[g])
        up(u_refs[2 * g + 1], halo_v_ref, wv_ref, groups[g])

    def gated(g):
        gate = conv(u_refs[2 * g], cwg_ref, cbg_ref, groups[g])
        val = conv(u_refs[2 * g + 1], cwv_ref, cbv_ref, groups[g])
        return (gate * jax.nn.sigmoid(gate) * val).astype(BF16)

    def down(g, act):
        for lo in range(0, D_MODEL, FFN_OUT_BLOCK):
            out_cols = slice(lo, lo + FFN_OUT_BLOCK)
            o_ref[:, out_cols] += _dot(act, wd_ref[groups[g], out_cols])

    up_both(0)
    if cast_next:
        for s_ref, d_ref in zip(src[:3], dst[:3]):
            d_ref[...] = s_ref[...].astype(BF16)
    for g in range(len(groups)):
        act = gated(g)
        if g + 1 < len(groups):
            up_both(g + 1)
        down(g, act)

    if final_norm:
        @pl.when(f == pl.num_programs(1) - 1)
        def _():
            o_ref[...] = _rms_scale(o_ref[...], fgain_ref[...])


def _ffn(x, gain, w_up, conv_w, conv_b, w_down, layer, final_gain, next_weights, *, tm, tf,
         final_norm):
    s = x.shape[0]
    ni, nf = s // tm, FFN_DIM // tf
    cast_next = next_weights is not None
    kernel = functools.partial(_ffn_kernel, tm=tm, final_norm=final_norm, cast_next=cast_next)
    in_specs = [
        pl.BlockSpec((tm, D_MODEL), lambda i, f: (i, 0)),
        pl.BlockSpec((1, D_MODEL), lambda i, f: (0, 0)),
        pl.BlockSpec((D_MODEL, tf), lambda i, f: (0, f)),
        pl.BlockSpec((D_MODEL, tf), lambda i, f: (0, nf + f)),
        pl.BlockSpec((None, CONV_WIDTH, tf), lambda i, f: (layer, 0, f)),
        pl.BlockSpec((None, CONV_WIDTH, tf), lambda i, f: (layer, 0, nf + f)),
        pl.BlockSpec((None, 1, tf), lambda i, f: (layer, 0, f)),
        pl.BlockSpec((None, 1, tf), lambda i, f: (layer, 0, nf + f)),
        pl.BlockSpec((tf, D_MODEL), lambda i, f: (f, 0)),
        pl.BlockSpec((1, D_MODEL), lambda i, f: (0, 0)),
    ]
    out_specs = [pl.BlockSpec((tm, D_MODEL), lambda i, f: (i, 0))]
    out_shape = [jax.ShapeDtypeStruct((s, D_MODEL), F32)]
    operands = [x, gain, w_up, w_up, conv_w, conv_w, conv_b, conv_b, w_down, final_gain]
    if cast_next:
        def tiling(w, rows_over_f):
            rows, cols = w.shape[1:]
            nr, nc = (nf, ni) if rows_over_f else (ni, nf)
            assert rows % nr == 0 and cols % nc == 0, (w.shape, ni, nf)
            if rows_over_f:
                return (rows // nr, cols // nc), lambda i, f: (f, i)
            return (rows // nr, cols // nc), lambda i, f: (i, f)

        w_in, w_up_next, w_down_next, w_out = next_weights
        for w, rows_over_f in ((w_in, False), (w_up_next, False), (w_down_next, True)):
            block, index = tiling(w, rows_over_f)
            in_specs.append(pl.BlockSpec((None,) + block,
                                         lambda i, f, index=index: (layer + 1,) + index(i, f)))
            out_specs.append(pl.BlockSpec(block, index))
            out_shape.append(jax.ShapeDtypeStruct(w.shape[1:], BF16))
        block = (w_out.shape[1] // ni, w_out.shape[2])
        in_specs.append(pl.BlockSpec((None,) + block, lambda i, f: (layer + 1, i, 0)))
        out_specs.append(pl.BlockSpec(block, lambda i, f: (i, 0)))
        out_shape.append(jax.ShapeDtypeStruct(w_out.shape[1:], BF16))
        operands += [w_in, w_up_next, w_down_next, w_out]
    return pl.pallas_call(
        kernel,
        grid=(ni, nf),
        in_specs=in_specs,
        out_specs=out_specs,
        out_shape=out_shape,
        scratch_shapes=[
            pltpu.VMEM((tm, D_MODEL), BF16),
            pltpu.VMEM((nf, SUBLANES, tf), F32),
            pltpu.VMEM((nf, SUBLANES, tf), F32),
        ] + [pltpu.VMEM((tm + SUBLANES, FFN_GROUP), F32)] * (2 * tf // FFN_GROUP),
        compiler_params=_params(("arbitrary", "arbitrary")),
        name="ffn",
    )(*operands)


def _rope_tables(seq):
    def tables(width, reps):
        rot = width // ROPE_FRACTION
        half = rot // 2
        inv = 1.0 / (ROPE_THETA ** (jnp.arange(0, rot, 2, dtype=F32) / rot))
        ang = jnp.arange(seq, dtype=F32)[:, None] * inv[None, :]
        cos, sin = jnp.cos(ang), jnp.sin(ang)
        zeros = lambda n: jnp.zeros((seq, n), F32)
        keep = jnp.concatenate([cos, cos, jnp.ones((seq, width - rot), F32)], axis=1)
        plus = jnp.concatenate([-sin, zeros(width - half)], axis=1)
        minus = jnp.concatenate([zeros(half), sin, zeros(width - rot)], axis=1)
        return jnp.stack([jnp.tile(t, (1, reps)) for t in (keep, plus, minus)])

    return jnp.stack([tables(HEAD_DIM, 1), tables(DIFF_QK_DIM, 2)])


def kernel(x, norm_mix, w_in, lambda_q1, lambda_k1, lambda_q2, lambda_k2, diff_subln,
           sgu_ln_g, sgu_ln_b, sgu_w, sgu_b, w_out, norm_ffn, w_up, conv_w, conv_b, w_down,
           norm_final):
    batch, seq, _ = x.shape
    depth = w_in.shape[0]
    tables = _rope_tables(seq)
    row = lambda v: v.reshape(1, -1)
    conv_b = conv_b[:, None, :]
    outs = []
    for bi in range(batch):
        xs = x[bi]
        wi = w_in[0].astype(BF16)
        for l in range(depth):
            last = l == depth - 1
            lambda_init = 0.8 - 0.6 * math.exp(-0.3 * l)
            out_c, h = _spatial_gating(xs, row(norm_mix[l]), wi,
                                       row(sgu_ln_g[l]), row(sgu_ln_b[l]), sgu_w[l],
                                       jnp.transpose(sgu_b[l]), tm=512)
            slabs_a, k_slabs_b, qv_t_b = _in_proj(h, wi, tables, tm=1024, t=DIFF_T)
            out_a = _dilated_mixture(slabs_a)
            lam_vecs = jnp.stack([lambda_q1[l], lambda_k1[l], lambda_q2[l], lambda_k2[l]])
            out_b, *cast = _diff_attention(k_slabs_b, qv_t_b, lam_vecs, row(diff_subln[l]),
                                           (w_up, w_down, w_out) if l == 0 else (),
                                           t=DIFF_T, lambda_init=lambda_init)
            if l == 0:
                wu, wd, wo = cast
            xs = _out_proj(xs, out_a, out_b, out_c, wo, tm=512)
            xs, *cast = _ffn(xs, row(norm_ffn[l]), wu, conv_w, conv_b, wd, l, row(norm_final),
                             None if last else (w_in, w_up, w_down, w_out),
                             tm=512, tf=512, final_norm=last)
            if not last:
                wi, wu, wd, wo = cast
        outs.append(xs)
    return jnp.stack(outs)
```

```python
import functools
import math

import jax
import jax.numpy as jnp
from jax import lax
from jax.experimental import pallas as pl
from jax.experimental.pallas import tpu as pltpu

F32 = jnp.float32
BF16 = jnp.bfloat16

D_MODEL = 2048
HEAD_DIM = 128
N_HEADS_A = 6
N_HEADS_B = 6
N_HEADS_C = 4
DIFF_QK_DIM = 64
WIDTH_A = N_HEADS_A * HEAD_DIM
WIDTH_B = N_HEADS_B * HEAD_DIM
WIDTH_C = N_HEADS_C * HEAD_DIM
ATTN_COLS = 3 * WIDTH_A + 3 * WIDTH_B
DILATIONS = (1, 4, 16)
BLK = 128
SUPER = BLK * DILATIONS[-1]
CHUNK = 128
ROPE_THETA = 500000.0
ROPE_FRACTION = 4
FFN_DIM = 5632
CONV_WIDTH = 3
EPS = 1e-6
LOG2E = math.log2(math.e)
MASKED = -1e30

LANES = 128
SUBLANES = 8
VMEM_LIMIT = 56 * 1024 * 1024
DIFF_T = 1024
DIFF_KEYS = 256
DIFF_AHEAD = 3
DENOM_ROWS = 16
IN_PROJ_BLOCK = 256
FFN_GROUP = 256
FFN_OUT_BLOCK = 512
DILATED_SOFTMAX_LAG = 2
DILATED_VALUE_LAG = 4


def _params(semantics, vmem=VMEM_LIMIT):
    return pltpu.CompilerParams(dimension_semantics=semantics, vmem_limit_bytes=vmem)


def _rms_scale(x, gain):
    return x * lax.rsqrt(jnp.mean(x * x, axis=-1, keepdims=True) + EPS) * gain


def _dot_nt(a, b):
    return lax.dot_general(a, b, (((1,), (1,)), ((), ())), preferred_element_type=F32)


def _dot(a, b):
    return jnp.dot(a, b, preferred_element_type=F32)


def _in_proj_kernel(h_ref, w_ref, tab_ref, oa_ref, obk_ref, obt_ref, *, t):
    j = pl.program_id(1)

    lane = lax.broadcasted_iota(jnp.int32, (h_ref.shape[0], LANES), 1)

    def rotate(xh, width):
        half = width // ROPE_FRACTION // 2
        first_half = (lane & (width - 1)) < half
        partner = jnp.where(first_half, pltpu.roll(xh, LANES - half, 1), pltpu.roll(xh, half, 1))
        return xh * tab_ref[0, 0] + partner * tab_ref[0, 1]

    rotate_a = lambda xh: rotate(xh, HEAD_DIM)
    rotate_b = lambda xh: rotate(xh, DIFF_QK_DIM)

    identity = lambda xh: xh

    def store(out_ref):
        def fn(hh, y):
            out_ref[hh] = y.astype(out_ref.dtype)
        return fn

    def store_transposed(hh, y):
        for c in range(y.shape[0] // t):
            obt_ref[hh, c] = y[c * t:(c + 1) * t, :].T.astype(obt_ref.dtype)

    def group(fn, put):
        heads_per_block = IN_PROJ_BLOCK // HEAD_DIM
        blocks = [slice(lo, lo + IN_PROJ_BLOCK) for lo in range(0, WIDTH_A, IN_PROJ_BLOCK)]
        project = lambda cols: _dot(h_ref[...], w_ref[:, cols])
        pending = project(blocks[0])
        for bi in range(len(blocks)):
            acc = pending
            if bi + 1 < len(blocks):
                pending = project(blocks[bi + 1])
            for hb in range(heads_per_block):
                put(bi * heads_per_block + hb, fn(acc[:, hb * HEAD_DIM:(hb + 1) * HEAD_DIM]))

    @pl.when(j < 2)
    def _():
        group(rotate_a, store(oa_ref))

    @pl.when(j == 2)
    def _():
        group(identity, store(oa_ref))

    @pl.when(j == 3)
    def _():
        group(rotate_b, store_transposed)

    @pl.when(j == 4)
    def _():
        group(rotate_b, store(obk_ref))

    @pl.when(j == 5)
    def _():
        group(identity, store_transposed)


def _in_proj(h, w_in, tables, *, tm, t):
    s = h.shape[0]
    grid = (s // tm, 6)
    return pl.pallas_call(
        functools.partial(_in_proj_kernel, t=t),
        grid=grid,
        in_specs=[
            pl.BlockSpec((tm, D_MODEL), lambda i, j: (i, 0)),
            pl.BlockSpec((D_MODEL, WIDTH_A), lambda i, j: (0, j)),
            pl.BlockSpec((1, 2, tm, LANES), lambda i, j: (j // 3, 0, i, 0)),
        ],
        out_specs=[
            pl.BlockSpec((6, tm, LANES), lambda i, j: (jnp.minimum(j, 2), i, 0)),
            pl.BlockSpec((6, tm, LANES), lambda i, j: (0, i, 0)),
            pl.BlockSpec((6, tm // t, HEAD_DIM, t), lambda i, j: (j // 5, i, 0, 0)),
        ],
        out_shape=[
            jax.ShapeDtypeStruct((18, s, LANES), F32),
            jax.ShapeDtypeStruct((6, s, LANES), BF16),
            jax.ShapeDtypeStruct((12, s // t, HEAD_DIM, t), BF16),
        ],
        compiler_params=_params(("arbitrary", "arbitrary")),
        name="in_proj",
    )(h, w_in, tables)


def _dilated_kernel(q_ref, k_ref, v_ref, o_ref, m_ref, l_ref, acc_ref, kc_ref, vc_ref):
    @pl.when(pl.program_id(1) == 0)
    def _():
        kc_ref[...] = jnp.zeros(kc_ref.shape, BF16)
        vc_ref[...] = jnp.zeros(vc_ref.shape, BF16)

    row = lax.broadcasted_iota(jnp.int32, (BLK, 2 * BLK), 0)
    col = lax.broadcasted_iota(jnp.int32, (BLK, 2 * BLK), 1)
    window = jnp.where(col < BLK, col - row, row - col + BLK) >= 0
    no_prev = jnp.where(pl.program_id(1) > 0, 0, BLK)
    first_window = jnp.where(col < BLK, col - row - no_prev, row - col + BLK) >= 0
    scale = HEAD_DIM ** -0.5 * LOG2E
    ones = jnp.ones((2 * BLK, LANES), BF16)

    def rows(d, start):
        return pl.ds(start, BLK) if d == 1 else pl.ds(start, BLK, stride=d)

    chains = [(d, r) for d in DILATIONS for r in range(d)]

    def stage_scores(unit, carry):
        pi, d, r, bb = unit
        chain = chains.index((d, r))
        sl = rows(d, r + d * BLK * bb)
        q = (q_ref[0, sl, :] * scale).astype(BF16)
        k_cur = k_ref[0, sl, :].astype(BF16)
        v_cur = v_ref[0, sl, :].astype(BF16)
        if bb == 0:
            k_prev, v_prev = kc_ref[chain], vc_ref[chain]
        else:
            k_prev, v_prev = carry
        if bb == SUPER // (BLK * d) - 1:
            kc_ref[chain] = k_cur
            vc_ref[chain] = v_cur
        scores = _dot_nt(q, jnp.concatenate([k_prev, k_cur], axis=0))
        scores = jnp.where(first_window if bb == 0 else window, scores, MASKED)
        values = jnp.concatenate([jnp.concatenate([v_prev, v_cur], axis=0), ones], axis=1)
        return dict(pi=pi, sl=sl, scores=scores, values=values), (k_cur, v_cur)

    def stage_softmax(state):
        m_blk = jnp.max(state["scores"], axis=1, keepdims=True)
        state["probs"] = jnp.exp2(state.pop("scores") - m_blk).astype(BF16)
        m_ref[state["pi"], state["sl"], :] = jnp.broadcast_to(m_blk, (BLK, LANES))

    def stage_values(state):
        out = _dot(state["probs"], state["values"])
        acc_ref[state["pi"], state["sl"], :] = out[:, :HEAD_DIM]
        l_ref[state["pi"], state["sl"], :] = out[:, HEAD_DIM:]

    units = [(pi, d, r, bb) for pi, d in enumerate(DILATIONS) for r in range(d)
             for bb in range(SUPER // (BLK * d))]
    states = [None] * len(units)
    carry = None
    for i in range(len(units) + DILATED_VALUE_LAG):
        if i < len(units):
            states[i], carry = stage_scores(units[i], carry)
        if 0 <= i - DILATED_SOFTMAX_LAG < len(units):
            stage_softmax(states[i - DILATED_SOFTMAX_LAG])
        if 0 <= i - DILATED_VALUE_LAG < len(units):
            stage_values(states[i - DILATED_VALUE_LAG])
            states[i - DILATED_VALUE_LAG] = None

    for lo in range(0, SUPER, BLK):
        sl = slice(lo, lo + BLK)
        m_all = functools.reduce(jnp.maximum, [m_ref[pi, sl, :] for pi in range(len(DILATIONS))])
        num = den = 0.0
        for pi in range(len(DILATIONS)):
            w = jnp.exp2(m_ref[pi, sl, :] - m_all)
            num = num + w * acc_ref[pi, sl, :]
            den = den + w * l_ref[pi, sl, :]
        o_ref[sl, :] = (num / den).astype(o_ref.dtype)


def _dilated_mixture(slabs_a):
    s = slabs_a.shape[1]

    blk = (1, SUPER, LANES)
    return pl.pallas_call(
        _dilated_kernel,
        grid=(N_HEADS_A, s // SUPER),
        in_specs=[
            pl.BlockSpec(blk, lambda h, sb: (h, sb, 0)),
            pl.BlockSpec(blk, lambda h, sb: (N_HEADS_A + h, sb, 0)),
            pl.BlockSpec(blk, lambda h, sb: (2 * N_HEADS_A + h, sb, 0)),
        ],
        out_specs=pl.BlockSpec((SUPER, HEAD_DIM), lambda h, sb: (sb, h)),
        out_shape=jax.ShapeDtypeStruct((s, WIDTH_A), BF16),
        scratch_shapes=([pltpu.VMEM((len(DILATIONS), SUPER, LANES), F32)] * 3
                        + [pltpu.VMEM((sum(DILATIONS), BLK, HEAD_DIM), BF16)] * 2),
        compiler_params=_params(("arbitrary", "arbitrary")),
        name="dilated",
    )(slabs_a, slabs_a, slabs_a)


def _diff_attn_kernel(lam_ref, gain_ref, qt_ref, k_ref, vt_ref, *rest, t, lambda_init, cast_steps):
    qi = pl.program_id(1)
    n_cast = (len(rest) - 4) // 2
    src, rest = rest[:n_cast], rest[n_cast:]
    o_ref, dst = rest[0], rest[1:1 + n_cast]
    qs_ref, m_ref, acc_ref = rest[1 + n_cast:]
    if n_cast:
        @pl.when(pl.program_id(0) * pl.num_programs(1) + qi < cast_steps)
        def _():
            for s_ref, d_ref in zip(src, dst):
                d_ref[...] = s_ref[...].astype(BF16)

    kb = DIFF_KEYS
    nsub = t // kb
    qt = qt_ref[0, 0].astype(F32) * (DIFF_QK_DIM ** -0.5 * LOG2E)
    dim = lax.broadcasted_iota(jnp.int32, (HEAD_DIM, kb), 0)
    for b in range(nsub):
        blk = qt[:, b * kb:(b + 1) * kb]
        qs_ref[:, 2 * b * kb:(2 * b + 1) * kb] = jnp.where(dim < DIFF_QK_DIM, blk, 0.0).astype(BF16)
        qs_ref[:, (2 * b + 1) * kb:(2 * b + 2) * kb] = (
            jnp.where(dim >= DIFF_QK_DIM, blk, 0.0).astype(BF16))
    m_ref[...] = jnp.full(m_ref.shape, MASKED, F32)
    acc_ref[...] = jnp.zeros(acc_ref.shape, F32)
    ones = jnp.ones((DENOM_ROWS, kb), BF16)

    def chunk(c, diagonal):
        def first_col(j):
            return 2 * j * kb if diagonal else 0

        def scores(j):
            kc = k_ref[0, pl.ds(pl.multiple_of(c * t + j * kb, kb), kb), :]
            return _dot(kc, qs_ref[:, first_col(j):])

        pending = [scores(j) for j in range(min(DIFF_AHEAD, nsub))]
        for j in range(nsub):
            st = pending.pop(0)
            if j + DIFF_AHEAD < nsub:
                pending.append(scores(j + DIFF_AHEAD))
            lo = first_col(j)
            if diagonal:
                head = st[:, :2 * kb]
                key = lax.broadcasted_iota(jnp.int32, head.shape, 0)
                col = lax.broadcasted_iota(jnp.int32, head.shape, 1)
                head = jnp.where(key <= jnp.where(col >= kb, col - kb, col), head, MASKED)
                st = head if j == nsub - 1 else jnp.concatenate([head, st[:, 2 * kb:]], axis=1)
            m_old = m_ref[:, lo:]
            m_new = jnp.maximum(m_old, jnp.max(st, axis=0, keepdims=True))
            alpha = jnp.exp2(m_old - m_new)
            p = jnp.exp2(st - m_new).astype(BF16)
            vt_aug = jnp.concatenate([vt_ref[0, c, :, j * kb:(j + 1) * kb], ones], axis=0)
            acc_ref[:, lo:] = alpha * acc_ref[:, lo:] + _dot(vt_aug, p)
            m_ref[:, lo:] = m_new

    def body(c, carry):
        chunk(c, False)
        return carry

    lax.fori_loop(0, qi, body, 0)
    chunk(qi, True)

    o = acc_ref[0:HEAD_DIM, :] / acc_ref[HEAD_DIM:HEAD_DIM + 1, :]
    o1 = jnp.concatenate([o[:, 2 * b * kb:(2 * b + 1) * kb] for b in range(nsub)], axis=1)
    o2 = jnp.concatenate([o[:, (2 * b + 1) * kb:(2 * b + 2) * kb] for b in range(nsub)], axis=1)
    lam = (jnp.exp(jnp.sum(lam_ref[0:1, :] * lam_ref[1:2, :], axis=1, keepdims=True))
           - jnp.exp(jnp.sum(lam_ref[2:3, :] * lam_ref[3:4, :], axis=1, keepdims=True))
           + lambda_init)
    ob = o1 - lam * o2
    ob = ob * lax.rsqrt(jnp.mean(ob * ob, axis=0, keepdims=True) + EPS)
    o_ref[...] = (ob.T * gain_ref[...] * (1.0 - lambda_init)).astype(o_ref.dtype)


def _diff_attention(k_slabs, qv_t, lam_vecs, gain, cast=(), cast_layer=0, *, t, lambda_init):
    s = k_slabs.shape[1]
    nq = s // t
    cast_steps = 1
    while (2 * cast_steps <= N_HEADS_B * nq
           and all(w.shape[1] % (2 * cast_steps * 2 * SUBLANES) == 0 for w in cast)):
        cast_steps *= 2
    kernel = functools.partial(_diff_attn_kernel, t=t, lambda_init=lambda_init,
                               cast_steps=cast_steps)
    in_specs = [
        pl.BlockSpec((4, DIFF_QK_DIM), lambda h, qi: (0, 0)),
        pl.BlockSpec((1, HEAD_DIM), lambda h, qi: (0, 0)),
        pl.BlockSpec((1, 1, HEAD_DIM, t), lambda h, qi: (h, qi, 0, 0)),
        pl.BlockSpec((1, s, LANES), lambda h, qi: (h, 0, 0)),
        pl.BlockSpec((1, s // t, HEAD_DIM, t), lambda h, qi: (N_HEADS_B + h, 0, 0, 0)),
    ]
    out_specs = [pl.BlockSpec((t, HEAD_DIM), lambda h, qi: (qi, h))]
    out_shape = [jax.ShapeDtypeStruct((s, WIDTH_B), BF16)]
    block_of = lambda h, qi: jnp.minimum(h * nq + qi, cast_steps - 1)
    for w in cast:
        block = (w.shape[1] // cast_steps, w.shape[2])
        in_specs.append(pl.BlockSpec((None,) + block,
                                     lambda h, qi: (cast_layer, block_of(h, qi), 0)))
        out_specs.append(pl.BlockSpec(block, lambda h, qi: (block_of(h, qi), 0)))
        out_shape.append(jax.ShapeDtypeStruct(w.shape[1:], BF16))
    return pl.pallas_call(
        kernel,
        grid=(N_HEADS_B, nq),
        in_specs=in_specs,
        out_specs=out_specs,
        out_shape=out_shape,
        scratch_shapes=[
            pltpu.VMEM((HEAD_DIM, 2 * t), BF16),
            pltpu.VMEM((1, 2 * t), F32),
            pltpu.VMEM((HEAD_DIM + DENOM_ROWS, 2 * t), F32),
        ],
        compiler_params=_params(("arbitrary", "arbitrary")),
        name="diff_attn",
    )(lam_vecs, gain, qv_t, k_slabs, qv_t, *cast)


def _sgu_kernel(x_ref, gain_ref, wu_ref, wv_ref, lng_ref, lnb_ref, ws_ref, bs_ref, o_ref, h_ref,
                *, tm):
    h = _rms_scale(x_ref[...], gain_ref[...]).astype(BF16)
    h_ref[...] = h
    gelu = lambda z: 0.5 * z * (1.0 + lax.erf(z * math.sqrt(0.5)))
    u = gelu(_dot(h, wu_ref[...]))
    v = gelu(_dot(h, wv_ref[...]))
    mu = jnp.mean(v, axis=-1, keepdims=True)
    var = jnp.mean(jnp.square(v - mu), axis=-1, keepdims=True)
    vn = ((v - mu) * lax.rsqrt(var + EPS) * lng_ref[...] + lnb_ref[...]).astype(BF16)
    row = lax.broadcasted_iota(jnp.int32, (CHUNK, CHUNK), 0)
    col = lax.broadcasted_iota(jnp.int32, (CHUNK, CHUNK), 1)
    for g in range(N_HEADS_C):
        wm = jnp.where(col <= row, ws_ref[g], 0.0).astype(BF16)
        bias = bs_ref[:, g:g + 1]
        cols = slice(g * HEAD_DIM, (g + 1) * HEAD_DIM)
        for c in range(tm // CHUNK):
            rows = slice(c * CHUNK, (c + 1) * CHUNK)
            y = _dot(wm, vn[rows, cols]) + bias
            o_ref[rows, cols] = (u[rows, cols] * y).astype(o_ref.dtype)


def _spatial_gating(x, gain, w_in, ln_g, ln_b, w_s, b_s_t, *, tm):
    s = x.shape[0]
    const2 = lambda i: (0, 0)
    return pl.pallas_call(
        functools.partial(_sgu_kernel, tm=tm),
        grid=(s // tm,),
        in_specs=[
            pl.BlockSpec((tm, D_MODEL), lambda i: (i, 0)),
            pl.BlockSpec((1, D_MODEL), const2),
            pl.BlockSpec((D_MODEL, WIDTH_C), lambda i: (0, ATTN_COLS // WIDTH_C)),
            pl.BlockSpec((D_MODEL, WIDTH_C), lambda i: (0, ATTN_COLS // WIDTH_C + 1)),
            pl.BlockSpec((1, WIDTH_C), const2),
            pl.BlockSpec((1, WIDTH_C), const2),
            pl.BlockSpec((N_HEADS_C, CHUNK, CHUNK), lambda i: (0, 0, 0)),
            pl.BlockSpec((CHUNK, N_HEADS_C), const2),
        ],
        out_specs=[pl.BlockSpec((tm, WIDTH_C), lambda i: (i, 0)),
                   pl.BlockSpec((tm, D_MODEL), lambda i: (i, 0))],
        out_shape=[jax.ShapeDtypeStruct((s, WIDTH_C), BF16),
                   jax.ShapeDtypeStruct((s, D_MODEL), BF16)],
        compiler_params=_params(("arbitrary",)),
        name="sgu",
    )(x, gain, w_in, w_in, ln_g, ln_b, w_s, b_s_t)


def _out_proj_kernel(x_ref, a_ref, b_ref, c_ref, wa_ref, wb_ref, wc_ref, o_ref):
    o_ref[...] = (x_ref[...] + _dot(a_ref[...], wa_ref[...]) + _dot(b_ref[...], wb_ref[...])
                  + _dot(c_ref[...], wc_ref[...]))


def _out_proj(x, a, b, c, w_out, *, tm):
    s = x.shape[0]
    return pl.pallas_call(
        _out_proj_kernel,
        grid=(s // tm,),
        in_specs=[
            pl.BlockSpec((tm, D_MODEL), lambda i: (i, 0)),
            pl.BlockSpec((tm, WIDTH_A), lambda i: (i, 0)),
            pl.BlockSpec((tm, WIDTH_B), lambda i: (i, 0)),
            pl.BlockSpec((tm, WIDTH_C), lambda i: (i, 0)),
            pl.BlockSpec((WIDTH_A, D_MODEL), lambda i: (0, 0)),
            pl.BlockSpec((WIDTH_B, D_MODEL), lambda i: (1, 0)),
            pl.BlockSpec((WIDTH_C, D_MODEL), lambda i: ((WIDTH_A + WIDTH_B) // WIDTH_C, 0)),
        ],
        out_specs=pl.BlockSpec((tm, D_MODEL), lambda i: (i, 0)),
        out_shape=jax.ShapeDtypeStruct((s, D_MODEL), F32),
        compiler_params=_params(("arbitrary",)),
        name="out_proj",
    )(x, a, b, c, w_out, w_out, w_out)


def _ffn_kernel(x_ref, gain_ref, wg_ref, wv_ref, cwg_ref, cwv_ref, cbg_ref, cbv_ref, wd_ref,
                fgain_ref, *rest, tm, final_norm, cast_next):
    i = pl.program_id(0)
    f = pl.program_id(1)
    if cast_next:
        src, rest = rest[:4], rest[4:]
        o_ref, dst, rest = rest[0], rest[1:5], rest[5:]

        @pl.when(f == 0)
        def _():
            dst[3][...] = src[3][...].astype(BF16)
    else:
        o_ref, rest = rest[0], rest[1:]
    h_ref, halo_g_ref, halo_v_ref, *u_refs = rest

    @pl.when(f == 0)
    def _():
        h_ref[...] = _rms_scale(x_ref[...], gain_ref[...]).astype(BF16)
        o_ref[...] = x_ref[...]

    @pl.when(i == 0)
    def _():
        halo_g_ref[f] = jnp.zeros(halo_g_ref.shape[1:], F32)
        halo_v_ref[f] = jnp.zeros(halo_v_ref.shape[1:], F32)

    def up(u_ref, halo_ref, w_up_ref, cols):
        u_ref[0:SUBLANES, :] = halo_ref[f, :, cols]
        u_ref[SUBLANES:SUBLANES + tm, :] = _dot(h_ref[...], w_up_ref[:, cols])
        halo_ref[f, :, cols] = u_ref[tm:tm + SUBLANES, :]

    def conv(u_ref, cw_ref, cb_ref, cols):
        out = cb_ref[:, cols]
        for tap in range(CONV_WIDTH):
            lo = SUBLANES - (CONV_WIDTH - 1) + tap
            out = out + cw_ref[tap:tap + 1, cols] * u_ref[lo:lo + tm, :]
        return out

    tf = wd_ref.shape[0]
    groups = [slice(lo, lo + FFN_GROUP) for lo in range(0, tf, FFN_GROUP)]
    def up_both(g):
        up(u_refs[2 * g], halo_g_ref, wg_ref, groups[g])
        up(u_refs[2 * g + 1], halo_v_ref, wv_ref, groups[g])

    def gated(g):
        gate = conv(u_refs[2 * g], cwg_ref, cbg_ref, groups[g])
        val = conv(u_refs[2 * g + 1], cwv_ref, cbv_ref, groups[g])
        return (gate * jax.nn.sigmoid(gate) * val).astype(BF16)

    def down(g, act):
        for lo in range(0, D_MODEL, FFN_OUT_BLOCK):
            out_cols = slice(lo, lo + FFN_OUT_BLOCK)
            o_ref[:, out_cols] += _dot(act, wd_ref[groups[g], out_cols])

    up_both(0)
    if cast_next:
        for s_ref, d_ref in zip(src[:3], dst[:3]):
            d_ref[...] = s_ref[...].astype(BF16)
    for g in range(len(groups)):
        act = gated(g)
        if g + 1 < len(groups):
            up_both(g + 1)
        down(g, act)

    if final_norm:
        @pl.when(f == pl.num_programs(1) - 1)
        def _():
            o_ref[...] = _rms_scale(o_ref[...], fgain_ref[...])


def _ffn(x, gain, w_up, conv_w, conv_b, w_down, layer, final_gain, next_weights, *, tm, tf,
         final_norm):
    s = x.shape[0]
    ni, nf = s // tm, FFN_DIM // tf
    cast_next = next_weights is not None
    kernel = functools.partial(_ffn_kernel, tm=tm, final_norm=final_norm, cast_next=cast_next)
    in_specs = [
        pl.BlockSpec((tm, D_MODEL), lambda i, f: (i, 0)),
        pl.BlockSpec((1, D_MODEL), lambda i, f: (0, 0)),
        pl.BlockSpec((D_MODEL, tf), lambda i, f: (0, f)),
        pl.BlockSpec((D_MODEL, tf), lambda i, f: (0, nf + f)),
        pl.BlockSpec((None, CONV_WIDTH, tf), lambda i, f: (layer, 0, f)),
        pl.BlockSpec((None, CONV_WIDTH, tf), lambda i, f: (layer, 0, nf + f)),
        pl.BlockSpec((None, 1, tf), lambda i, f: (layer, 0, f)),
        pl.BlockSpec((None, 1, tf), lambda i, f: (layer, 0, nf + f)),
        pl.BlockSpec((tf, D_MODEL), lambda i, f: (f, 0)),
        pl.BlockSpec((1, D_MODEL), lambda i, f: (0, 0)),
    ]
    out_specs = [pl.BlockSpec((tm, D_MODEL), lambda i, f: (i, 0))]
    out_shape = [jax.ShapeDtypeStruct((s, D_MODEL), F32)]
    operands = [x, gain, w_up, w_up, conv_w, conv_w, conv_b, conv_b, w_down, final_gain]
    if cast_next:
        def tiling(w, rows_over_f):
            rows, cols = w.shape[1:]
            nr, nc = (nf, ni) if rows_over_f else (ni, nf)
            assert rows % nr == 0 and cols % nc == 0, (w.shape, ni, nf)
            if rows_over_f:
                return (rows // nr, cols // nc), lambda i, f: (f, i)
            return (rows // nr, cols // nc), lambda i, f: (i, f)

        w_in, w_up_next, w_down_next, w_out = next_weights
        for w, rows_over_f in ((w_in, False), (w_up_next, False), (w_down_next, True)):
            block, index = tiling(w, rows_over_f)
            in_specs.append(pl.BlockSpec((None,) + block,
                                         lambda i, f, index=index: (layer + 1,) + index(i, f)))
            out_specs.append(pl.BlockSpec(block, index))
            out_shape.append(jax.ShapeDtypeStruct(w.shape[1:], BF16))
        block = (w_out.shape[1] // ni, w_out.shape[2])
        in_specs.append(pl.BlockSpec((None,) + block, lambda i, f: (layer + 1, i, 0)))
        out_specs.append(pl.BlockSpec(block, lambda i, f: (i, 0)))
        out_shape.append(jax.ShapeDtypeStruct(w_out.shape[1:], BF16))
        operands += [w_in, w_up_next, w_down_next, w_out]
    return pl.pallas_call(
        kernel,
        grid=(ni, nf),
        in_specs=in_specs,
        out_specs=out_specs,
        out_shape=out_shape,
        scratch_shapes=[
            pltpu.VMEM((tm, D_MODEL), BF16),
            pltpu.VMEM((nf, SUBLANES, tf), F32),
            pltpu.VMEM((nf, SUBLANES, tf), F32),
        ] + [pltpu.VMEM((tm + SUBLANES, FFN_GROUP), F32)] * (2 * tf // FFN_GROUP),
        compiler_params=_params(("arbitrary", "arbitrary")),
        name="ffn",
    )(*operands)


def _rope_tables(seq):
    def tables(width, reps):
        rot = width // ROPE_FRACTION
        inv = 1.0 / (ROPE_THETA ** (jnp.arange(0, rot, 2, dtype=F32) / rot))
        ang = jnp.arange(seq, dtype=F32)[:, None] * inv[None, :]
        cos, sin = jnp.cos(ang), jnp.sin(ang)
        keep = jnp.concatenate([cos, cos, jnp.ones((seq, width - rot), F32)], axis=1)
        signed = jnp.concatenate([-sin, sin, jnp.zeros((seq, width - rot), F32)], axis=1)
        return jnp.stack([jnp.tile(t, (1, reps)) for t in (keep, signed)])

    return jnp.stack([tables(HEAD_DIM, 1), tables(DIFF_QK_DIM, 2)])


def kernel(x, norm_mix, w_in, lambda_q1, lambda_k1, lambda_q2, lambda_k2, diff_subln,
           sgu_ln_g, sgu_ln_b, sgu_w, sgu_b, w_out, norm_ffn, w_up, conv_w, conv_b, w_down,
           norm_final):
    batch, seq, _ = x.shape
    depth = w_in.shape[0]
    tables = _rope_tables(seq)
    row = lambda v: v.reshape(1, -1)
    conv_b = conv_b[:, None, :]
    outs = []
    for bi in range(batch):
        xs = x[bi]
        wi = w_in[0].astype(BF16)
        for l in range(depth):
            last = l == depth - 1
            lambda_init = 0.8 - 0.6 * math.exp(-0.3 * l)
            out_c, h = _spatial_gating(xs, row(norm_mix[l]), wi,
                                       row(sgu_ln_g[l]), row(sgu_ln_b[l]), sgu_w[l],
                                       jnp.transpose(sgu_b[l]), tm=1024)
            slabs_a, k_slabs_b, qv_t_b = _in_proj(h, wi, tables, tm=1024, t=DIFF_T)
            out_a = _dilated_mixture(slabs_a)
            lam_vecs = jnp.stack([lambda_q1[l], lambda_k1[l], lambda_q2[l], lambda_k2[l]])
            out_b, *cast = _diff_attention(k_slabs_b, qv_t_b, lam_vecs, row(diff_subln[l]),
                                           (w_up, w_down, w_out) if l == 0 else (),
                                           t=DIFF_T, lambda_init=lambda_init)
            if l == 0:
                wu, wd, wo = cast
            xs = _out_proj(xs, out_a, out_b, out_c, wo, tm=512)
            xs, *cast = _ffn(xs, row(norm_ffn[l]), wu, conv_w, conv_b, wd, l, row(norm_final),
                             None if last else (w_in, w_up, w_down, w_out),
                             tm=512, tf=512, final_norm=last)
            if not last:
                wi, wu, wd, wo = cast
        outs.append(xs)
    return jnp.stack(outs)
```

```python
import functools
import math

import jax
import jax.numpy as jnp
from jax import lax
from jax.experimental import pallas as pl
from jax.experimental.pallas import tpu as pltpu

F32 = jnp.float32
BF16 = jnp.bfloat16

D_MODEL = 2048
HEAD_DIM = 128
N_HEADS_A = 6
N_HEADS_B = 6
N_HEADS_C = 4
DIFF_QK_DIM = 64
WIDTH_A = N_HEADS_A * HEAD_DIM
WIDTH_B = N_HEADS_B * HEAD_DIM
WIDTH_C = N_HEADS_C * HEAD_DIM
ATTN_COLS = 3 * WIDTH_A + 3 * WIDTH_B
DILATIONS = (1, 4, 16)
BLK = 128
SUPER = BLK * DILATIONS[-1]
CHUNK = 128
ROPE_THETA = 500000.0
ROPE_FRACTION = 4
FFN_DIM = 5632
CONV_WIDTH = 3
EPS = 1e-6
LOG2E = math.log2(math.e)
MASKED = -1e30

LANES = 128
SUBLANES = 8
VMEM_LIMIT = 56 * 1024 * 1024
DIFF_T = 1024
DIFF_KEYS = 256
DIFF_AHEAD = 3
DENOM_ROWS = 16
IN_PROJ_BLOCK = 256
FFN_GROUP = 256
FFN_OUT_BLOCK = 512
DILATED_SOFTMAX_LAG = 2
DILATED_VALUE_LAG = 4


def _params(semantics, vmem=VMEM_LIMIT):
    return pltpu.CompilerParams(dimension_semantics=semantics, vmem_limit_bytes=vmem)


def _rms_scale(x, gain):
    return x * lax.rsqrt(jnp.mean(x * x, axis=-1, keepdims=True) + EPS) * gain


def _dot_nt(a, b):
    return lax.dot_general(a, b, (((1,), (1,)), ((), ())), preferred_element_type=F32)


def _dot(a, b):
    return jnp.dot(a, b, preferred_element_type=F32)


def _in_proj_kernel(h_ref, w_ref, tab_ref, oa_ref, obk_ref, obt_ref, *, t):
    j = pl.program_id(1)

    lane = lax.broadcasted_iota(jnp.int32, (h_ref.shape[0], LANES), 1)

    def rotate(xh, width):
        half = width // ROPE_FRACTION // 2
        first_half = (lane & (width - 1)) < half
        partner = jnp.where(first_half, pltpu.roll(xh, LANES - half, 1), pltpu.roll(xh, half, 1))
        return xh * tab_ref[0, 0] + partner * tab_ref[0, 1]

    rotate_a = lambda xh: rotate(xh, HEAD_DIM)
    rotate_b = lambda xh: rotate(xh, DIFF_QK_DIM)

    identity = lambda xh: xh

    def store(out_ref):
        def fn(hh, y):
            out_ref[hh] = y.astype(out_ref.dtype)
        return fn

    def store_transposed(hh, y):
        for c in range(y.shape[0] // t):
            obt_ref[hh, c] = y[c * t:(c + 1) * t, :].T.astype(obt_ref.dtype)

    def group(fn, put):
        heads_per_block = IN_PROJ_BLOCK // HEAD_DIM
        blocks = [slice(lo, lo + IN_PROJ_BLOCK) for lo in range(0, WIDTH_A, IN_PROJ_BLOCK)]
        project = lambda cols: _dot(h_ref[...], w_ref[:, cols])
        pending = project(blocks[0])
        for bi in range(len(blocks)):
            acc = pending
            if bi + 1 < len(blocks):
                pending = project(blocks[bi + 1])
            for hb in range(heads_per_block):
                put(bi * heads_per_block + hb, fn(acc[:, hb * HEAD_DIM:(hb + 1) * HEAD_DIM]))

    @pl.when(j < 2)
    def _():
        group(rotate_a, store(oa_ref))

    @pl.when(j == 2)
    def _():
        group(identity, store(oa_ref))

    @pl.when(j == 3)
    def _():
        group(rotate_b, store_transposed)

    @pl.when(j == 4)
    def _():
        group(rotate_b, store(obk_ref))

    @pl.when(j == 5)
    def _():
        group(identity, store_transposed)


def _in_proj(h, w_in, tables, *, tm, t):
    s = h.shape[0]
    grid = (s // tm, 6)
    return pl.pallas_call(
        functools.partial(_in_proj_kernel, t=t),
        grid=grid,
        in_specs=[
            pl.BlockSpec((tm, D_MODEL), lambda i, j: (i, 0)),
            pl.BlockSpec((D_MODEL, WIDTH_A), lambda i, j: (0, j)),
            pl.BlockSpec((1, 2, tm, LANES), lambda i, j: (j // 3, 0, i, 0)),
        ],
        out_specs=[
            pl.BlockSpec((6, tm, LANES), lambda i, j: (jnp.minimum(j, 2), i, 0)),
            pl.BlockSpec((6, tm, LANES), lambda i, j: (0, i, 0)),
            pl.BlockSpec((6, tm // t, HEAD_DIM, t), lambda i, j: (j // 5, i, 0, 0)),
        ],
        out_shape=[
            jax.ShapeDtypeStruct((18, s, LANES), F32),
            jax.ShapeDtypeStruct((6, s, LANES), BF16),
            jax.ShapeDtypeStruct((12, s // t, HEAD_DIM, t), BF16),
        ],
        compiler_params=_params(("arbitrary", "arbitrary")),
        name="in_proj",
    )(h, w_in, tables)


def _dilated_kernel(q_ref, k_ref, v_ref, o_ref, m_ref, l_ref, acc_ref, kc_ref, vc_ref):
    @pl.when(pl.program_id(1) == 0)
    def _():
        kc_ref[...] = jnp.zeros(kc_ref.shape, BF16)
        vc_ref[...] = jnp.zeros(vc_ref.shape, BF16)

    row = lax.broadcasted_iota(jnp.int32, (BLK, 2 * BLK), 0)
    col = lax.broadcasted_iota(jnp.int32, (BLK, 2 * BLK), 1)
    window = jnp.where(col < BLK, col - row, row - col + BLK) >= 0
    no_prev = jnp.where(pl.program_id(1) > 0, 0, BLK)
    first_window = jnp.where(col < BLK, col - row - no_prev, row - col + BLK) >= 0
    scale = HEAD_DIM ** -0.5 * LOG2E
    ones = jnp.ones((2 * BLK, LANES), BF16)

    def rows(d, start):
        return pl.ds(start, BLK) if d == 1 else pl.ds(start, BLK, stride=d)

    chains = [(d, r) for d in DILATIONS for r in range(d)]

    def stage_scores(unit, carry):
        pi, d, r, bb = unit
        chain = chains.index((d, r))
        sl = rows(d, r + d * BLK * bb)
        q = (q_ref[0, sl, :] * scale).astype(BF16)
        k_cur = k_ref[0, sl, :].astype(BF16)
        v_cur = v_ref[0, sl, :].astype(BF16)
        if bb == 0:
            k_prev, v_prev = kc_ref[chain], vc_ref[chain]
        else:
            k_prev, v_prev = carry
        if bb == SUPER // (BLK * d) - 1:
            kc_ref[chain] = k_cur
            vc_ref[chain] = v_cur
        scores = _dot_nt(q, jnp.concatenate([k_prev, k_cur], axis=0))
        scores = jnp.where(first_window if bb == 0 else window, scores, MASKED)
        values = jnp.concatenate([jnp.concatenate([v_prev, v_cur], axis=0), ones], axis=1)
        return dict(pi=pi, sl=sl, scores=scores, values=values), (k_cur, v_cur)

    def stage_softmax(state):
        m_blk = jnp.max(state["scores"], axis=1, keepdims=True)
        state["probs"] = jnp.exp2(state.pop("scores") - m_blk).astype(BF16)
        m_ref[state["pi"], state["sl"], :] = jnp.broadcast_to(m_blk, (BLK, LANES))

    def stage_values(state):
        out = _dot(state["probs"], state["values"])
        acc_ref[state["pi"], state["sl"], :] = out[:, :HEAD_DIM]
        l_ref[state["pi"], state["sl"], :] = out[:, HEAD_DIM:]

    units = [(pi, d, r, bb) for pi, d in enumerate(DILATIONS) for r in range(d)
             for bb in range(SUPER // (BLK * d))]
    states = [None] * len(units)
    carry = None
    for i in range(len(units) + DILATED_VALUE_LAG):
        if i < len(units):
            states[i], carry = stage_scores(units[i], carry)
        if 0 <= i - DILATED_SOFTMAX_LAG < len(units):
            stage_softmax(states[i - DILATED_SOFTMAX_LAG])
        if 0 <= i - DILATED_VALUE_LAG < len(units):
            stage_values(states[i - DILATED_VALUE_LAG])
            states[i - DILATED_VALUE_LAG] = None

    for lo in range(0, SUPER, BLK):
        sl = slice(lo, lo + BLK)
        m_all = functools.reduce(jnp.maximum, [m_ref[pi, sl, :] for pi in range(len(DILATIONS))])
        num = den = 0.0
        for pi in range(len(DILATIONS)):
            w = jnp.exp2(m_ref[pi, sl, :] - m_all)
            num = num + w * acc_ref[pi, sl, :]
            den = den + w * l_ref[pi, sl, :]
        o_ref[sl, :] = (num / den).astype(o_ref.dtype)


def _dilated_mixture(slabs_a):
    s = slabs_a.shape[1]

    blk = (1, SUPER, LANES)
    return pl.pallas_call(
        _dilated_kernel,
        grid=(N_HEADS_A, s // SUPER),
        in_specs=[
            pl.BlockSpec(blk, lambda h, sb: (h, sb, 0)),
            pl.BlockSpec(blk, lambda h, sb: (N_HEADS_A + h, sb, 0)),
            pl.BlockSpec(blk, lambda h, sb: (2 * N_HEADS_A + h, sb, 0)),
        ],
        out_specs=pl.BlockSpec((SUPER, HEAD_DIM), lambda h, sb: (sb, h)),
        out_shape=jax.ShapeDtypeStruct((s, WIDTH_A), BF16),
        scratch_shapes=([pltpu.VMEM((len(DILATIONS), SUPER, LANES), F32)] * 3
                        + [pltpu.VMEM((sum(DILATIONS), BLK, HEAD_DIM), BF16)] * 2),
        compiler_params=_params(("arbitrary", "arbitrary")),
        name="dilated",
    )(slabs_a, slabs_a, slabs_a)


def _diff_attn_kernel(lam_ref, gain_ref, qt_ref, k_ref, vt_ref, *rest, t, lambda_init, cast_steps):
    qi = pl.program_id(1)
    n_cast = (len(rest) - 4) // 2
    src, rest = rest[:n_cast], rest[n_cast:]
    o_ref, dst = rest[0], rest[1:1 + n_cast]
    qs_ref, m_ref, acc_ref = rest[1 + n_cast:]
    if n_cast:
        @pl.when(pl.program_id(0) * pl.num_programs(1) + qi < cast_steps)
        def _():
            for s_ref, d_ref in zip(src, dst):
                d_ref[...] = s_ref[...].astype(BF16)

    kb = DIFF_KEYS
    nsub = t // kb
    qt = qt_ref[0, 0].astype(F32) * (DIFF_QK_DIM ** -0.5 * LOG2E)
    dim = lax.broadcasted_iota(jnp.int32, (HEAD_DIM, kb), 0)
    for b in range(nsub):
        blk = qt[:, b * kb:(b + 1) * kb]
        qs_ref[:, 2 * b * kb:(2 * b + 1) * kb] = jnp.where(dim < DIFF_QK_DIM, blk, 0.0).astype(BF16)
        qs_ref[:, (2 * b + 1) * kb:(2 * b + 2) * kb] = (
            jnp.where(dim >= DIFF_QK_DIM, blk, 0.0).astype(BF16))
    m_ref[...] = jnp.full(m_ref.shape, MASKED, F32)
    acc_ref[...] = jnp.zeros(acc_ref.shape, F32)
    ones = jnp.ones((DENOM_ROWS, kb), BF16)

    def chunk(c, diagonal):
        def first_col(j):
            return 2 * j * kb if diagonal else 0

        def scores(j):
            kc = k_ref[0, pl.ds(pl.multiple_of(c * t + j * kb, kb), kb), :]
            return _dot(kc, qs_ref[:, first_col(j):])

        pending = [scores(j) for j in range(min(DIFF_AHEAD, nsub))]
        for j in range(nsub):
            st = pending.pop(0)
            if j + DIFF_AHEAD < nsub:
                pending.append(scores(j + DIFF_AHEAD))
            lo = first_col(j)
            if diagonal:
                head = st[:, :2 * kb]
                key = lax.broadcasted_iota(jnp.int32, head.shape, 0)
                col = lax.broadcasted_iota(jnp.int32, head.shape, 1)
                head = jnp.where(key <= jnp.where(col >= kb, col - kb, col), head, MASKED)
                st = head if j == nsub - 1 else jnp.concatenate([head, st[:, 2 * kb:]], axis=1)
            m_old = m_ref[:, lo:]
            m_new = jnp.maximum(m_old, jnp.max(st, axis=0, keepdims=True))
            alpha = jnp.exp2(m_old - m_new)
            p = jnp.exp2(st - m_new).astype(BF16)
            vt_aug = jnp.concatenate([vt_ref[0, c, :, j * kb:(j + 1) * kb], ones], axis=0)
            acc_ref[:, lo:] = alpha * acc_ref[:, lo:] + _dot(vt_aug, p)
            m_ref[:, lo:] = m_new

    def body(c, carry):
        chunk(c, False)
        return carry

    lax.fori_loop(0, qi, body, 0)
    chunk(qi, True)

    o = acc_ref[0:HEAD_DIM, :] / acc_ref[HEAD_DIM:HEAD_DIM + 1, :]
    o1 = jnp.concatenate([o[:, 2 * b * kb:(2 * b + 1) * kb] for b in range(nsub)], axis=1)
    o2 = jnp.concatenate([o[:, (2 * b + 1) * kb:(2 * b + 2) * kb] for b in range(nsub)], axis=1)
    lam = (jnp.exp(jnp.sum(lam_ref[0:1, :] * lam_ref[1:2, :], axis=1, keepdims=True))
           - jnp.exp(jnp.sum(lam_ref[2:3, :] * lam_ref[3:4, :], axis=1, keepdims=True))
           + lambda_init)
    ob = o1 - lam * o2
    ob = ob * lax.rsqrt(jnp.mean(ob * ob, axis=0, keepdims=True) + EPS)
    o_ref[...] = (ob.T * gain_ref[...] * (1.0 - lambda_init)).astype(o_ref.dtype)


def _diff_attention(k_slabs, qv_t, lam_vecs, gain, layer, cast=(), *, t, lambda_init):
    s = k_slabs.shape[1]
    nq = s // t
    cast_steps = 1
    while (2 * cast_steps <= N_HEADS_B * nq
           and all(w.shape[1] % (2 * cast_steps * 2 * SUBLANES) == 0 for w in cast)):
        cast_steps *= 2
    kernel = functools.partial(_diff_attn_kernel, t=t, lambda_init=lambda_init,
                               cast_steps=cast_steps)
    in_specs = [
        pl.BlockSpec((None, 4, DIFF_QK_DIM), lambda h, qi: (layer, 0, 0)),
        pl.BlockSpec((None, 1, HEAD_DIM), lambda h, qi: (layer, 0, 0)),
        pl.BlockSpec((1, 1, HEAD_DIM, t), lambda h, qi: (h, qi, 0, 0)),
        pl.BlockSpec((1, s, LANES), lambda h, qi: (h, 0, 0)),
        pl.BlockSpec((1, s // t, HEAD_DIM, t), lambda h, qi: (N_HEADS_B + h, 0, 0, 0)),
    ]
    out_specs = [pl.BlockSpec((t, HEAD_DIM), lambda h, qi: (qi, h))]
    out_shape = [jax.ShapeDtypeStruct((s, WIDTH_B), BF16)]
    block_of = lambda h, qi: jnp.minimum(h * nq + qi, cast_steps - 1)
    for w in cast:
        block = (w.shape[1] // cast_steps, w.shape[2])
        in_specs.append(pl.BlockSpec((None,) + block,
                                     lambda h, qi: (layer, block_of(h, qi), 0)))
        out_specs.append(pl.BlockSpec(block, lambda h, qi: (block_of(h, qi), 0)))
        out_shape.append(jax.ShapeDtypeStruct(w.shape[1:], BF16))
    return pl.pallas_call(
        kernel,
        grid=(N_HEADS_B, nq),
        in_specs=in_specs,
        out_specs=out_specs,
        out_shape=out_shape,
        scratch_shapes=[
            pltpu.VMEM((HEAD_DIM, 2 * t), BF16),
            pltpu.VMEM((1, 2 * t), F32),
            pltpu.VMEM((HEAD_DIM + DENOM_ROWS, 2 * t), F32),
        ],
        compiler_params=_params(("arbitrary", "arbitrary")),
        name="diff_attn",
    )(lam_vecs, gain, qv_t, k_slabs, qv_t, *cast)


def _sgu_kernel(x_ref, gain_ref, wu_ref, wv_ref, lng_ref, lnb_ref, ws_ref, bs_ref, o_ref, h_ref,
                *, tm):
    h = _rms_scale(x_ref[...], gain_ref[...]).astype(BF16)
    h_ref[...] = h
    gelu = lambda z: 0.5 * z * (1.0 + lax.erf(z * math.sqrt(0.5)))
    u = gelu(_dot(h, wu_ref[...]))
    v = gelu(_dot(h, wv_ref[...]))
    mu = jnp.mean(v, axis=-1, keepdims=True)
    var = jnp.mean(jnp.square(v - mu), axis=-1, keepdims=True)
    vn = ((v - mu) * lax.rsqrt(var + EPS) * lng_ref[...] + lnb_ref[...]).astype(BF16)
    row = lax.broadcasted_iota(jnp.int32, (CHUNK, CHUNK), 0)
    col = lax.broadcasted_iota(jnp.int32, (CHUNK, CHUNK), 1)
    for g in range(N_HEADS_C):
        wm = jnp.where(col <= row, ws_ref[g], 0.0).astype(BF16)
        bias = bs_ref[:, g:g + 1]
        cols = slice(g * HEAD_DIM, (g + 1) * HEAD_DIM)
        for c in range(tm // CHUNK):
            rows = slice(c * CHUNK, (c + 1) * CHUNK)
            y = _dot(wm, vn[rows, cols]) + bias
            o_ref[rows, cols] = (u[rows, cols] * y).astype(o_ref.dtype)


def _spatial_gating(x, gain, w_in, ln_g, ln_b, w_s, b_s_t, layer, *, tm):
    s = x.shape[0]
    per_layer = lambda i: (layer, 0, 0)
    return pl.pallas_call(
        functools.partial(_sgu_kernel, tm=tm),
        grid=(s // tm,),
        in_specs=[
            pl.BlockSpec((tm, D_MODEL), lambda i: (i, 0)),
            pl.BlockSpec((None, 1, D_MODEL), per_layer),
            pl.BlockSpec((D_MODEL, WIDTH_C), lambda i: (0, ATTN_COLS // WIDTH_C)),
            pl.BlockSpec((D_MODEL, WIDTH_C), lambda i: (0, ATTN_COLS // WIDTH_C + 1)),
            pl.BlockSpec((None, 1, WIDTH_C), per_layer),
            pl.BlockSpec((None, 1, WIDTH_C), per_layer),
            pl.BlockSpec((None, N_HEADS_C, CHUNK, CHUNK), lambda i: (layer, 0, 0, 0)),
            pl.BlockSpec((None, CHUNK, N_HEADS_C), per_layer),
        ],
        out_specs=[pl.BlockSpec((tm, WIDTH_C), lambda i: (i, 0)),
                   pl.BlockSpec((tm, D_MODEL), lambda i: (i, 0))],
        out_shape=[jax.ShapeDtypeStruct((s, WIDTH_C), BF16),
                   jax.ShapeDtypeStruct((s, D_MODEL), BF16)],
        compiler_params=_params(("arbitrary",)),
        name="sgu",
    )(x, gain, w_in, w_in, ln_g, ln_b, w_s, b_s_t)


def _out_proj_kernel(x_ref, a_ref, b_ref, c_ref, wa_ref, wb_ref, wc_ref, o_ref):
    o_ref[...] = (x_ref[...] + _dot(a_ref[...], wa_ref[...]) + _dot(b_ref[...], wb_ref[...])
                  + _dot(c_ref[...], wc_ref[...]))


def _out_proj(x, a, b, c, w_out, *, tm):
    s = x.shape[0]
    return pl.pallas_call(
        _out_proj_kernel,
        grid=(s // tm,),
        in_specs=[
            pl.BlockSpec((tm, D_MODEL), lambda i: (i, 0)),
            pl.BlockSpec((tm, WIDTH_A), lambda i: (i, 0)),
            pl.BlockSpec((tm, WIDTH_B), lambda i: (i, 0)),
            pl.BlockSpec((tm, WIDTH_C), lambda i: (i, 0)),
            pl.BlockSpec((WIDTH_A, D_MODEL), lambda i: (0, 0)),
            pl.BlockSpec((WIDTH_B, D_MODEL), lambda i: (1, 0)),
            pl.BlockSpec((WIDTH_C, D_MODEL), lambda i: ((WIDTH_A + WIDTH_B) // WIDTH_C, 0)),
        ],
        out_specs=pl.BlockSpec((tm, D_MODEL), lambda i: (i, 0)),
        out_shape=jax.ShapeDtypeStruct((s, D_MODEL), F32),
        compiler_params=_params(("arbitrary",)),
        name="out_proj",
    )(x, a, b, c, w_out, w_out, w_out)


def _ffn_kernel(x_ref, gain_ref, wg_ref, wv_ref, cwg_ref, cwv_ref, cbg_ref, cbv_ref, wd_ref,
                fgain_ref, *rest, tm, final_norm, cast_next):
    i = pl.program_id(0)
    f = pl.program_id(1)
    if cast_next:
        src, rest = rest[:4], rest[4:]
        o_ref, dst, rest = rest[0], rest[1:5], rest[5:]

        @pl.when(f == 0)
        def _():
            dst[3][...] = src[3][...].astype(BF16)
    else:
        o_ref, rest = rest[0], rest[1:]
    h_ref, halo_g_ref, halo_v_ref, *u_refs = rest

    @pl.when(f == 0)
    def _():
        h_ref[...] = _rms_scale(x_ref[...], gain_ref[...]).astype(BF16)
        o_ref[...] = x_ref[...]

    @pl.when(i == 0)
    def _():
        halo_g_ref[f] = jnp.zeros(halo_g_ref.shape[1:], F32)
        halo_v_ref[f] = jnp.zeros(halo_v_ref.shape[1:], F32)

    def up(u_ref, halo_ref, w_up_ref, cols):
        u_ref[0:SUBLANES, :] = halo_ref[f, :, cols]
        u_ref[SUBLANES:SUBLANES + tm, :] = _dot(h_ref[...], w_up_ref[:, cols])
        halo_ref[f, :, cols] = u_ref[tm:tm + SUBLANES, :]

    def conv(u_ref, cw_ref, cb_ref, cols):
        out = cb_ref[:, cols]
        for tap in range(CONV_WIDTH):
            lo = SUBLANES - (CONV_WIDTH - 1) + tap
            out = out + cw_ref[tap:tap + 1, cols] * u_ref[lo:lo + tm, :]
        return out

    tf = wd_ref.shape[0]
    groups = [slice(lo, lo + FFN_GROUP) for lo in range(0, tf, FFN_GROUP)]
    def up_both(g):
        up(u_refs[2 * g], halo_g_ref, wg_ref, groups[g])
        up(u_refs[2 * g + 1], halo_v_ref, wv_ref, groups[g])

    def gated(g):
        gate = conv(u_refs[2 * g], cwg_ref, cbg_ref, groups[g])
        val = conv(u_refs[2 * g + 1], cwv_ref, cbv_ref, groups[g])
        return (gate * jax.nn.sigmoid(gate) * val).astype(BF16)

    def down(g, act):
        for lo in range(0, D_MODEL, FFN_OUT_BLOCK):
            out_cols = slice(lo, lo + FFN_OUT_BLOCK)
            o_ref[:, out_cols] += _dot(act, wd_ref[groups[g], out_cols])

    up_both(0)
    if cast_next:
        for s_ref, d_ref in zip(src[:3], dst[:3]):
            d_ref[...] = s_ref[...].astype(BF16)
    for g in range(len(groups)):
        act = gated(g)
        if g + 1 < len(groups):
            up_both(g + 1)
        down(g, act)

    if final_norm:
        @pl.when(f == pl.num_programs(1) - 1)
        def _():
            o_ref[...] = _rms_scale(o_ref[...], fgain_ref[...])


def _ffn(x, gain, w_up, conv_w, conv_b, w_down, layer, final_gain, next_weights, *, tm, tf,
         final_norm):
    s = x.shape[0]
    ni, nf = s // tm, FFN_DIM // tf
    cast_next = next_weights is not None
    kernel = functools.partial(_ffn_kernel, tm=tm, final_norm=final_norm, cast_next=cast_next)
    in_specs = [
        pl.BlockSpec((tm, D_MODEL), lambda i, f: (i, 0)),
        pl.BlockSpec((None, 1, D_MODEL), lambda i, f: (layer, 0, 0)),
        pl.BlockSpec((D_MODEL, tf), lambda i, f: (0, f)),
        pl.BlockSpec((D_MODEL, tf), lambda i, f: (0, nf + f)),
        pl.BlockSpec((None, CONV_WIDTH, tf), lambda i, f: (layer, 0, f)),
        pl.BlockSpec((None, CONV_WIDTH, tf), lambda i, f: (layer, 0, nf + f)),
        pl.BlockSpec((None, 1, tf), lambda i, f: (layer, 0, f)),
        pl.BlockSpec((None, 1, tf), lambda i, f: (layer, 0, nf + f)),
        pl.BlockSpec((tf, D_MODEL), lambda i, f: (f, 0)),
        pl.BlockSpec((1, D_MODEL), lambda i, f: (0, 0)),
    ]
    out_specs = [pl.BlockSpec((tm, D_MODEL), lambda i, f: (i, 0))]
    out_shape = [jax.ShapeDtypeStruct((s, D_MODEL), F32)]
    operands = [x, gain, w_up, w_up, conv_w, conv_w, conv_b, conv_b, w_down, final_gain]
    if cast_next:
        def tiling(w, rows_over_f):
            rows, cols = w.shape[1:]
            nr, nc = (nf, ni) if rows_over_f else (ni, nf)
            assert rows % nr == 0 and cols % nc == 0, (w.shape, ni, nf)
            if rows_over_f:
                return (rows // nr, cols // nc), lambda i, f: (f, i)
            return (rows // nr, cols // nc), lambda i, f: (i, f)

        w_in, w_up_next, w_down_next, w_out = next_weights
        for w, rows_over_f in ((w_in, False), (w_up_next, False), (w_down_next, True)):
            block, index = tiling(w, rows_over_f)
            in_specs.append(pl.BlockSpec((None,) + block,
                                         lambda i, f, index=index: (layer + 1,) + index(i, f)))
            out_specs.append(pl.BlockSpec(block, index))
            out_shape.append(jax.ShapeDtypeStruct(w.shape[1:], BF16))
        block = (w_out.shape[1] // ni, w_out.shape[2])
        in_specs.append(pl.BlockSpec((None,) + block, lambda i, f: (layer + 1, i, 0)))
        out_specs.append(pl.BlockSpec(block, lambda i, f: (i, 0)))
        out_shape.append(jax.ShapeDtypeStruct(w_out.shape[1:], BF16))
        operands += [w_in, w_up_next, w_down_next, w_out]
    return pl.pallas_call(
        kernel,
        grid=(ni, nf),
        in_specs=in_specs,
        out_specs=out_specs,
        out_shape=out_shape,
        scratch_shapes=[
            pltpu.VMEM((tm, D_MODEL), BF16),
            pltpu.VMEM((nf, SUBLANES, tf), F32),
            pltpu.VMEM((nf, SUBLANES, tf), F32),
        ] + [pltpu.VMEM((tm + SUBLANES, FFN_GROUP), F32)] * (2 * tf // FFN_GROUP),
        compiler_params=_params(("arbitrary", "arbitrary")),
        name="ffn",
    )(*operands)


def _rope_tables(seq):
    def tables(width, reps):
        rot = width // ROPE_FRACTION
        inv = 1.0 / (ROPE_THETA ** (jnp.arange(0, rot, 2, dtype=F32) / rot))
        ang = jnp.arange(seq, dtype=F32)[:, None] * inv[None, :]
        cos, sin = jnp.cos(ang), jnp.sin(ang)
        keep = jnp.concatenate([cos, cos, jnp.ones((seq, width - rot), F32)], axis=1)
        signed = jnp.concatenate([-sin, sin, jnp.zeros((seq, width - rot), F32)], axis=1)
        return jnp.stack([jnp.tile(t, (1, reps)) for t in (keep, signed)])

    return jnp.stack([tables(HEAD_DIM, 1), tables(DIFF_QK_DIM, 2)])


def kernel(x, norm_mix, w_in, lambda_q1, lambda_k1, lambda_q2, lambda_k2, diff_subln,
           sgu_ln_g, sgu_ln_b, sgu_w, sgu_b, w_out, norm_ffn, w_up, conv_w, conv_b, w_down,
           norm_final):
    batch, seq, _ = x.shape
    depth = w_in.shape[0]
    tables = _rope_tables(seq)
    rows = lambda v: v[:, None, :]
    norm_mix, norm_ffn, diff_subln, sgu_ln_g, sgu_ln_b, conv_b = map(
        rows, (norm_mix, norm_ffn, diff_subln, sgu_ln_g, sgu_ln_b, conv_b))
    lam_vecs = jnp.stack([lambda_q1, lambda_k1, lambda_q2, lambda_k2], axis=1)
    sgu_b_t = jnp.transpose(sgu_b, (0, 2, 1))
    outs = []
    for bi in range(batch):
        xs = x[bi]
        wi = w_in[0].astype(BF16)
        for l in range(depth):
            last = l == depth - 1
            lambda_init = 0.8 - 0.6 * math.exp(-0.3 * l)
            out_c, h = _spatial_gating(xs, norm_mix, wi, sgu_ln_g, sgu_ln_b, sgu_w, sgu_b_t, l,
                                       tm=1024)
            slabs_a, k_slabs_b, qv_t_b = _in_proj(h, wi, tables, tm=1024, t=DIFF_T)
            out_a = _dilated_mixture(slabs_a)
            out_b, *cast = _diff_attention(k_slabs_b, qv_t_b, lam_vecs, diff_subln, l,
                                           (w_up, w_down, w_out) if l == 0 else (),
                                           t=DIFF_T, lambda_init=lambda_init)
            if l == 0:
                wu, wd, wo = cast
            xs = _out_proj(xs, out_a, out_b, out_c, wo, tm=512)
            xs, *cast = _ffn(xs, norm_ffn, wu, conv_w, conv_b, wd, l, norm_final.reshape(1, -1),
                             None if last else (w_in, w_up, w_down, w_out),
                             tm=512, tf=512, final_norm=last)
            if not last:
                wi, wu, wd, wo = cast
        outs.append(xs)
    return jnp.stack(outs)
```

```python
import functools
import math

import jax
import jax.numpy as jnp
from jax import lax
from jax.experimental import pallas as pl
from jax.experimental.pallas import tpu as pltpu

F32 = jnp.float32
BF16 = jnp.bfloat16

D_MODEL = 2048
HEAD_DIM = 128
N_HEADS_A = 6
N_HEADS_B = 6
N_HEADS_C = 4
DIFF_QK_DIM = 64
WIDTH_A = N_HEADS_A * HEAD_DIM
WIDTH_B = N_HEADS_B * HEAD_DIM
WIDTH_C = N_HEADS_C * HEAD_DIM
ATTN_COLS = 3 * WIDTH_A + 3 * WIDTH_B
DILATIONS = (1, 4, 16)
BLK = 128
SUPER = BLK * DILATIONS[-1]
CHUNK = 128
ROPE_THETA = 500000.0
ROPE_FRACTION = 4
FFN_DIM = 5632
CONV_WIDTH = 3
EPS = 1e-6
LOG2E = math.log2(math.e)
MASKED = -1e30

LANES = 128
SUBLANES = 8
VMEM_LIMIT = 56 * 1024 * 1024
DIFF_T = 1024
DIFF_KEYS = 256
DIFF_AHEAD = 3
DENOM_ROWS = 16
IN_PROJ_BLOCK = 256
FFN_GROUP = 256
FFN_OUT_BLOCK = 512
DILATED_SOFTMAX_LAG = 2
DILATED_VALUE_LAG = 4


def _params(semantics, vmem=VMEM_LIMIT):
    return pltpu.CompilerParams(dimension_semantics=semantics, vmem_limit_bytes=vmem)


def _rms_scale(x, gain):
    return x * lax.rsqrt(jnp.mean(x * x, axis=-1, keepdims=True) + EPS) * gain


def _dot_nt(a, b):
    return lax.dot_general(a, b, (((1,), (1,)), ((), ())), preferred_element_type=F32)


def _dot(a, b):
    return jnp.dot(a, b, preferred_element_type=F32)


def _in_proj_kernel(h_ref, w_ref, tab_ref, oa_ref, obk_ref, obt_ref, *, t):
    j = pl.program_id(1)

    lane = lax.broadcasted_iota(jnp.int32, (h_ref.shape[0], LANES), 1)

    def rotate(xh, width):
        half = width // ROPE_FRACTION // 2
        first_half = (lane & (width - 1)) < half
        partner = jnp.where(first_half, pltpu.roll(xh, LANES - half, 1), pltpu.roll(xh, half, 1))
        return xh * tab_ref[0, 0] + partner * tab_ref[0, 1]

    rotate_a = lambda xh: rotate(xh, HEAD_DIM)
    rotate_b = lambda xh: rotate(xh, DIFF_QK_DIM)

    identity = lambda xh: xh

    def store(out_ref):
        def fn(hh, y):
            out_ref[hh] = y.astype(out_ref.dtype)
        return fn

    def store_transposed(hh, y):
        for c in range(y.shape[0] // t):
            obt_ref[hh, c] = y[c * t:(c + 1) * t, :].T.astype(obt_ref.dtype)

    def group(fn, put):
        heads_per_block = IN_PROJ_BLOCK // HEAD_DIM
        blocks = [slice(lo, lo + IN_PROJ_BLOCK) for lo in range(0, WIDTH_A, IN_PROJ_BLOCK)]
        project = lambda cols: _dot(h_ref[...], w_ref[:, cols])
        pending = project(blocks[0])
        for bi in range(len(blocks)):
            acc = pending
            if bi + 1 < len(blocks):
                pending = project(blocks[bi + 1])
            for hb in range(heads_per_block):
                put(bi * heads_per_block + hb, fn(acc[:, hb * HEAD_DIM:(hb + 1) * HEAD_DIM]))

    @pl.when(j < 2)
    def _():
        group(rotate_a, store(oa_ref))

    @pl.when(j == 2)
    def _():
        group(identity, store(oa_ref))

    @pl.when(j == 3)
    def _():
        group(rotate_b, store_transposed)

    @pl.when(j == 4)
    def _():
        group(rotate_b, store(obk_ref))

    @pl.when(j == 5)
    def _():
        group(identity, store_transposed)


def _in_proj(h, w_in, tables, *, tm, t):
    s = h.shape[0]
    grid = (s // tm, 6)
    return pl.pallas_call(
        functools.partial(_in_proj_kernel, t=t),
        grid=grid,
        in_specs=[
            pl.BlockSpec((tm, D_MODEL), lambda i, j: (i, 0)),
            pl.BlockSpec((D_MODEL, WIDTH_A), lambda i, j: (0, j)),
            pl.BlockSpec((1, 2, tm, LANES), lambda i, j: (j // 3, 0, i, 0)),
        ],
        out_specs=[
            pl.BlockSpec((6, tm, LANES), lambda i, j: (jnp.minimum(j, 2), i, 0)),
            pl.BlockSpec((6, tm, LANES), lambda i, j: (0, i, 0)),
            pl.BlockSpec((6, tm // t, HEAD_DIM, t), lambda i, j: (j // 5, i, 0, 0)),
        ],
        out_shape=[
            jax.ShapeDtypeStruct((18, s, LANES), F32),
            jax.ShapeDtypeStruct((6, s, LANES), BF16),
            jax.ShapeDtypeStruct((12, s // t, HEAD_DIM, t), BF16),
        ],
        compiler_params=_params(("arbitrary", "arbitrary")),
        name="in_proj",
    )(h, w_in, tables)


def _dilated_kernel(q_ref, k_ref, v_ref, o_ref, m_ref, l_ref, acc_ref, kc_ref, vc_ref):
    @pl.when(pl.program_id(1) == 0)
    def _():
        kc_ref[...] = jnp.zeros(kc_ref.shape, BF16)
        vc_ref[...] = jnp.zeros(vc_ref.shape, BF16)

    row = lax.broadcasted_iota(jnp.int32, (BLK, 2 * BLK), 0)
    col = lax.broadcasted_iota(jnp.int32, (BLK, 2 * BLK), 1)
    window = jnp.where(col < BLK, col - row, row - col + BLK) >= 0
    no_prev = jnp.where(pl.program_id(1) > 0, 0, BLK)
    first_window = jnp.where(col < BLK, col - row - no_prev, row - col + BLK) >= 0
    scale = HEAD_DIM ** -0.5 * LOG2E
    ones = jnp.ones((2 * BLK, LANES), BF16)

    def rows(d, start):
        return pl.ds(start, BLK) if d == 1 else pl.ds(start, BLK, stride=d)

    chains = [(d, r) for d in DILATIONS for r in range(d)]

    def stage_scores(unit, carry):
        pi, d, r, bb = unit
        chain = chains.index((d, r))
        sl = rows(d, r + d * BLK * bb)
        q = (q_ref[0, sl, :] * scale).astype(BF16)
        k_cur = k_ref[0, sl, :].astype(BF16)
        v_cur = v_ref[0, sl, :].astype(BF16)
        if bb == 0:
            k_prev, v_prev = kc_ref[chain], vc_ref[chain]
        else:
            k_prev, v_prev = carry
        if bb == SUPER // (BLK * d) - 1:
            kc_ref[chain] = k_cur
            vc_ref[chain] = v_cur
        scores = _dot_nt(q, jnp.concatenate([k_prev, k_cur], axis=0))
        scores = jnp.where(first_window if bb == 0 else window, scores, MASKED)
        values = jnp.concatenate([jnp.concatenate([v_prev, v_cur], axis=0), ones], axis=1)
        return dict(pi=pi, sl=sl, scores=scores, values=values), (k_cur, v_cur)

    def stage_softmax(state):
        m_blk = jnp.max(state["scores"], axis=1, keepdims=True)
        state["probs"] = jnp.exp2(state.pop("scores") - m_blk).astype(BF16)
        m_ref[state["pi"], state["sl"], :] = jnp.broadcast_to(m_blk, (BLK, LANES))

    def stage_values(state):
        out = _dot(state["probs"], state["values"])
        acc_ref[state["pi"], state["sl"], :] = out[:, :HEAD_DIM]
        l_ref[state["pi"], state["sl"], :] = out[:, HEAD_DIM:]

    units = [(pi, d, r, bb) for pi, d in enumerate(DILATIONS) for r in range(d)
             for bb in range(SUPER // (BLK * d))]
    states = [None] * len(units)
    carry = None
    for i in range(len(units) + DILATED_VALUE_LAG):
        if i < len(units):
            states[i], carry = stage_scores(units[i], carry)
        if 0 <= i - DILATED_SOFTMAX_LAG < len(units):
            stage_softmax(states[i - DILATED_SOFTMAX_LAG])
        if 0 <= i - DILATED_VALUE_LAG < len(units):
            stage_values(states[i - DILATED_VALUE_LAG])
            states[i - DILATED_VALUE_LAG] = None

    for lo in range(0, SUPER, BLK):
        sl = slice(lo, lo + BLK)
        m_all = functools.reduce(jnp.maximum, [m_ref[pi, sl, :] for pi in range(len(DILATIONS))])
        num = den = 0.0
        for pi in range(len(DILATIONS)):
            w = jnp.exp2(m_ref[pi, sl, :] - m_all)
            num = num + w * acc_ref[pi, sl, :]
            den = den + w * l_ref[pi, sl, :]
        o_ref[sl, :] = (num / den).astype(o_ref.dtype)


def _dilated_mixture(slabs_a):
    s = slabs_a.shape[1]

    blk = (1, SUPER, LANES)
    return pl.pallas_call(
        _dilated_kernel,
        grid=(N_HEADS_A, s // SUPER),
        in_specs=[
            pl.BlockSpec(blk, lambda h, sb: (h, sb, 0)),
            pl.BlockSpec(blk, lambda h, sb: (N_HEADS_A + h, sb, 0)),
            pl.BlockSpec(blk, lambda h, sb: (2 * N_HEADS_A + h, sb, 0)),
        ],
        out_specs=pl.BlockSpec((SUPER, HEAD_DIM), lambda h, sb: (sb, h)),
        out_shape=jax.ShapeDtypeStruct((s, WIDTH_A), BF16),
        scratch_shapes=([pltpu.VMEM((len(DILATIONS), SUPER, LANES), F32)] * 3
                        + [pltpu.VMEM((sum(DILATIONS), BLK, HEAD_DIM), BF16)] * 2),
        compiler_params=_params(("arbitrary", "arbitrary")),
        name="dilated",
    )(slabs_a, slabs_a, slabs_a)


def _diff_attn_kernel(lam_ref, gain_ref, qt_ref, k_ref, vt_ref, *rest, t, lambda_init, cast_steps):
    qi = pl.program_id(1)
    n_cast = (len(rest) - 4) // 2
    src, rest = rest[:n_cast], rest[n_cast:]
    o_ref, dst = rest[0], rest[1:1 + n_cast]
    qs_ref, m_ref, acc_ref = rest[1 + n_cast:]
    if n_cast:
        @pl.when(pl.program_id(0) * pl.num_programs(1) + qi < cast_steps)
        def _():
            for s_ref, d_ref in zip(src, dst):
                d_ref[...] = s_ref[...].astype(BF16)

    kb = DIFF_KEYS
    nsub = t // kb
    qt = qt_ref[0, 0].astype(F32) * (DIFF_QK_DIM ** -0.5 * LOG2E)
    dim = lax.broadcasted_iota(jnp.int32, (HEAD_DIM, kb), 0)
    for b in range(nsub):
        blk = qt[:, b * kb:(b + 1) * kb]
        qs_ref[:, 2 * b * kb:(2 * b + 1) * kb] = jnp.where(dim < DIFF_QK_DIM, blk, 0.0).astype(BF16)
        qs_ref[:, (2 * b + 1) * kb:(2 * b + 2) * kb] = (
            jnp.where(dim >= DIFF_QK_DIM, blk, 0.0).astype(BF16))
    m_ref[...] = jnp.full(m_ref.shape, MASKED, F32)
    acc_ref[...] = jnp.zeros(acc_ref.shape, F32)
    ones = jnp.ones((DENOM_ROWS, kb), BF16)

    def chunk(c, diagonal):
        def first_col(j):
            return 2 * j * kb if diagonal else 0

        def scores(j):
            kc = k_ref[0, pl.ds(pl.multiple_of(c * t + j * kb, kb), kb), :]
            return _dot(kc, qs_ref[:, first_col(j):])

        pending = [scores(j) for j in range(min(DIFF_AHEAD, nsub))]
        for j in range(nsub):
            st = pending.pop(0)
            if j + DIFF_AHEAD < nsub:
                pending.append(scores(j + DIFF_AHEAD))
            lo = first_col(j)
            if diagonal:
                head = st[:, :2 * kb]
                key = lax.broadcasted_iota(jnp.int32, head.shape, 0)
                col = lax.broadcasted_iota(jnp.int32, head.shape, 1)
                head = jnp.where(key <= jnp.where(col >= kb, col - kb, col), head, MASKED)
                st = head if j == nsub - 1 else jnp.concatenate([head, st[:, 2 * kb:]], axis=1)
            m_old = m_ref[:, lo:]
            m_new = jnp.maximum(m_old, jnp.max(st, axis=0, keepdims=True))
            alpha = jnp.exp2(m_old - m_new)
            p = jnp.exp2(st - m_new).astype(BF16)
            vt_aug = jnp.concatenate([vt_ref[0, c, :, j * kb:(j + 1) * kb], ones], axis=0)
            acc_ref[:, lo:] = alpha * acc_ref[:, lo:] + _dot(vt_aug, p)
            m_ref[:, lo:] = m_new

    def body(c, carry):
        chunk(c, False)
        return carry

    lax.fori_loop(0, qi, body, 0)
    chunk(qi, True)

    o = acc_ref[0:HEAD_DIM, :] / acc_ref[HEAD_DIM:HEAD_DIM + 1, :]
    o1 = jnp.concatenate([o[:, 2 * b * kb:(2 * b + 1) * kb] for b in range(nsub)], axis=1)
    o2 = jnp.concatenate([o[:, (2 * b + 1) * kb:(2 * b + 2) * kb] for b in range(nsub)], axis=1)
    lam = (jnp.exp(jnp.sum(lam_ref[0:1, :] * lam_ref[1:2, :], axis=1, keepdims=True))
           - jnp.exp(jnp.sum(lam_ref[2:3, :] * lam_ref[3:4, :], axis=1, keepdims=True))
           + lambda_init)
    ob = o1 - lam * o2
    ob = ob * lax.rsqrt(jnp.mean(ob * ob, axis=0, keepdims=True) + EPS)
    o_ref[...] = (ob.T * gain_ref[...] * (1.0 - lambda_init)).astype(o_ref.dtype)


def _diff_attention(k_slabs, qv_t, lam_vecs, gain, layer, cast=(), *, t, lambda_init):
    s = k_slabs.shape[1]
    nq = s // t
    cast_steps = 1
    while (2 * cast_steps <= N_HEADS_B * nq
           and all(w.shape[1] % (2 * cast_steps * 2 * SUBLANES) == 0 for w in cast)):
        cast_steps *= 2
    kernel = functools.partial(_diff_attn_kernel, t=t, lambda_init=lambda_init,
                               cast_steps=cast_steps)
    in_specs = [
        pl.BlockSpec((None, 4, DIFF_QK_DIM), lambda h, qi: (layer, 0, 0)),
        pl.BlockSpec((None, 1, HEAD_DIM), lambda h, qi: (layer, 0, 0)),
        pl.BlockSpec((1, 1, HEAD_DIM, t), lambda h, qi: (h, qi, 0, 0)),
        pl.BlockSpec((1, s, LANES), lambda h, qi: (h, 0, 0)),
        pl.BlockSpec((1, s // t, HEAD_DIM, t), lambda h, qi: (N_HEADS_B + h, 0, 0, 0)),
    ]
    out_specs = [pl.BlockSpec((t, HEAD_DIM), lambda h, qi: (qi, h))]
    out_shape = [jax.ShapeDtypeStruct((s, WIDTH_B), BF16)]
    block_of = lambda h, qi: jnp.minimum(h * nq + qi, cast_steps - 1)
    for w in cast:
        block = (w.shape[1] // cast_steps, w.shape[2])
        in_specs.append(pl.BlockSpec((None,) + block,
                                     lambda h, qi: (layer, block_of(h, qi), 0)))
        out_specs.append(pl.BlockSpec(block, lambda h, qi: (block_of(h, qi), 0)))
        out_shape.append(jax.ShapeDtypeStruct(w.shape[1:], BF16))
    return pl.pallas_call(
        kernel,
        grid=(N_HEADS_B, nq),
        in_specs=in_specs,
        out_specs=out_specs,
        out_shape=out_shape,
        scratch_shapes=[
            pltpu.VMEM((HEAD_DIM, 2 * t), BF16),
            pltpu.VMEM((1, 2 * t), F32),
            pltpu.VMEM((HEAD_DIM + DENOM_ROWS, 2 * t), F32),
        ],
        compiler_params=_params(("arbitrary", "arbitrary")),
        name="diff_attn",
    )(lam_vecs, gain, qv_t, k_slabs, qv_t, *cast)


def _sgu_kernel(x_ref, gain_ref, wu_ref, wv_ref, lng_ref, lnb_ref, ws_ref, bs_ref, o_ref, h_ref,
                *, tm):
    h = _rms_scale(x_ref[...], gain_ref[...]).astype(BF16)
    h_ref[...] = h
    gelu = lambda z: 0.5 * z * (1.0 + lax.erf(z * math.sqrt(0.5)))
    u = gelu(_dot(h, wu_ref[...]))
    v = gelu(_dot(h, wv_ref[...]))
    mu = jnp.mean(v, axis=-1, keepdims=True)
    var = jnp.mean(jnp.square(v - mu), axis=-1, keepdims=True)
    vn = ((v - mu) * lax.rsqrt(var + EPS) * lng_ref[...] + lnb_ref[...]).astype(BF16)
    row = lax.broadcasted_iota(jnp.int32, (CHUNK, CHUNK), 0)
    col = lax.broadcasted_iota(jnp.int32, (CHUNK, CHUNK), 1)
    for g in range(N_HEADS_C):
        wm = jnp.where(col <= row, ws_ref[g], 0.0).astype(BF16)
        bias = bs_ref[:, g:g + 1]
        cols = slice(g * HEAD_DIM, (g + 1) * HEAD_DIM)
        for c in range(tm // CHUNK):
            rows = slice(c * CHUNK, (c + 1) * CHUNK)
            y = _dot(wm, vn[rows, cols]) + bias
            o_ref[rows, cols] = (u[rows, cols] * y).astype(o_ref.dtype)


def _spatial_gating(x, gain, w_in, ln_g, ln_b, w_s, b_s_t, layer, *, tm):
    s = x.shape[0]
    per_layer = lambda i: (layer, 0, 0)
    return pl.pallas_call(
        functools.partial(_sgu_kernel, tm=tm),
        grid=(s // tm,),
        in_specs=[
            pl.BlockSpec((tm, D_MODEL), lambda i: (i, 0)),
            pl.BlockSpec((None, 1, D_MODEL), per_layer),
            pl.BlockSpec((D_MODEL, WIDTH_C), lambda i: (0, ATTN_COLS // WIDTH_C)),
            pl.BlockSpec((D_MODEL, WIDTH_C), lambda i: (0, ATTN_COLS // WIDTH_C + 1)),
            pl.BlockSpec((None, 1, WIDTH_C), per_layer),
            pl.BlockSpec((None, 1, WIDTH_C), per_layer),
            pl.BlockSpec((None, N_HEADS_C, CHUNK, CHUNK), lambda i: (layer, 0, 0, 0)),
            pl.BlockSpec((None, CHUNK, N_HEADS_C), per_layer),
        ],
        out_specs=[pl.BlockSpec((tm, WIDTH_C), lambda i: (i, 0)),
                   pl.BlockSpec((tm, D_MODEL), lambda i: (i, 0))],
        out_shape=[jax.ShapeDtypeStruct((s, WIDTH_C), BF16),
                   jax.ShapeDtypeStruct((s, D_MODEL), BF16)],
        compiler_params=_params(("arbitrary",)),
        name="sgu",
    )(x, gain, w_in, w_in, ln_g, ln_b, w_s, b_s_t)


def _out_proj_kernel(x_ref, a_ref, b_ref, c_ref, wa_ref, wb_ref, wc_ref, o_ref):
    o_ref[...] = (x_ref[...] + _dot(a_ref[...], wa_ref[...]) + _dot(b_ref[...], wb_ref[...])
                  + _dot(c_ref[...], wc_ref[...]))


def _out_proj(x, a, b, c, w_out, *, tm):
    s = x.shape[0]
    return pl.pallas_call(
        _out_proj_kernel,
        grid=(s // tm,),
        in_specs=[
            pl.BlockSpec((tm, D_MODEL), lambda i: (i, 0)),
            pl.BlockSpec((tm, WIDTH_A), lambda i: (i, 0)),
            pl.BlockSpec((tm, WIDTH_B), lambda i: (i, 0)),
            pl.BlockSpec((tm, WIDTH_C), lambda i: (i, 0)),
            pl.BlockSpec((WIDTH_A, D_MODEL), lambda i: (0, 0), pipeline_mode=pl.Buffered(1)),
            pl.BlockSpec((WIDTH_B, D_MODEL), lambda i: (1, 0), pipeline_mode=pl.Buffered(1)),
            pl.BlockSpec((WIDTH_C, D_MODEL), lambda i: ((WIDTH_A + WIDTH_B) // WIDTH_C, 0),
                         pipeline_mode=pl.Buffered(1)),
        ],
        out_specs=pl.BlockSpec((tm, D_MODEL), lambda i: (i, 0)),
        out_shape=jax.ShapeDtypeStruct((s, D_MODEL), F32),
        compiler_params=_params(("arbitrary",)),
        name="out_proj",
    )(x, a, b, c, w_out, w_out, w_out)


def _ffn_kernel(x_ref, gain_ref, wg_ref, wv_ref, cwg_ref, cwv_ref, cbg_ref, cbv_ref, wd_ref,
                fgain_ref, *rest, tm, final_norm, cast_next):
    i = pl.program_id(0)
    f = pl.program_id(1)
    if cast_next:
        src, rest = rest[:4], rest[4:]
        o_ref, dst, rest = rest[0], rest[1:5], rest[5:]

        @pl.when(f == 0)
        def _():
            dst[3][...] = src[3][...].astype(BF16)
    else:
        o_ref, rest = rest[0], rest[1:]
    h_ref, halo_g_ref, halo_v_ref, *u_refs = rest

    @pl.when(f == 0)
    def _():
        h_ref[...] = _rms_scale(x_ref[...], gain_ref[...]).astype(BF16)
        o_ref[...] = x_ref[...]

    @pl.when(i == 0)
    def _():
        halo_g_ref[f] = jnp.zeros(halo_g_ref.shape[1:], F32)
        halo_v_ref[f] = jnp.zeros(halo_v_ref.shape[1:], F32)

    def up(u_ref, halo_ref, w_up_ref, cols):
        u_ref[0:SUBLANES, :] = halo_ref[f, :, cols]
        u_ref[SUBLANES:SUBLANES + tm, :] = _dot(h_ref[...], w_up_ref[:, cols])
        halo_ref[f, :, cols] = u_ref[tm:tm + SUBLANES, :]

    def conv(u_ref, cw_ref, cb_ref, cols):
        out = cb_ref[:, cols]
        for tap in range(CONV_WIDTH):
            lo = SUBLANES - (CONV_WIDTH - 1) + tap
            out = out + cw_ref[tap:tap + 1, cols] * u_ref[lo:lo + tm, :]
        return out

    tf = wd_ref.shape[0]
    groups = [slice(lo, lo + FFN_GROUP) for lo in range(0, tf, FFN_GROUP)]
    def up_both(g):
        up(u_refs[2 * g], halo_g_ref, wg_ref, groups[g])
        up(u_refs[2 * g + 1], halo_v_ref, wv_ref, groups[g])

    def gated(g):
        gate = conv(u_refs[2 * g], cwg_ref, cbg_ref, groups[g])
        val = conv(u_refs[2 * g + 1], cwv_ref, cbv_ref, groups[g])
        return (gate * jax.nn.sigmoid(gate) * val).astype(BF16)

    def down(g, act):
        for lo in range(0, D_MODEL, FFN_OUT_BLOCK):
            out_cols = slice(lo, lo + FFN_OUT_BLOCK)
            o_ref[:, out_cols] += _dot(act, wd_ref[groups[g], out_cols])

    up_both(0)
    if cast_next:
        for s_ref, d_ref in zip(src[:3], dst[:3]):
            d_ref[...] = s_ref[...].astype(BF16)
    for g in range(len(groups)):
        act = gated(g)
        if g + 1 < len(groups):
            up_both(g + 1)
        down(g, act)

    if final_norm:
        @pl.when(f == pl.num_programs(1) - 1)
        def _():
            o_ref[...] = _rms_scale(o_ref[...], fgain_ref[...])


def _ffn(x, gain, w_up, conv_w, conv_b, w_down, layer, final_gain, next_weights, *, tm, tf,
         final_norm):
    s = x.shape[0]
    ni, nf = s // tm, FFN_DIM // tf
    cast_next = next_weights is not None
    kernel = functools.partial(_ffn_kernel, tm=tm, final_norm=final_norm, cast_next=cast_next)
    in_specs = [
        pl.BlockSpec((tm, D_MODEL), lambda i, f: (i, 0)),
        pl.BlockSpec((None, 1, D_MODEL), lambda i, f: (layer, 0, 0)),
        pl.BlockSpec((D_MODEL, tf), lambda i, f: (0, f)),
        pl.BlockSpec((D_MODEL, tf), lambda i, f: (0, nf + f)),
        pl.BlockSpec((None, CONV_WIDTH, tf), lambda i, f: (layer, 0, f)),
        pl.BlockSpec((None, CONV_WIDTH, tf), lambda i, f: (layer, 0, nf + f)),
        pl.BlockSpec((None, 1, tf), lambda i, f: (layer, 0, f)),
        pl.BlockSpec((None, 1, tf), lambda i, f: (layer, 0, nf + f)),
        pl.BlockSpec((tf, D_MODEL), lambda i, f: (f, 0)),
        pl.BlockSpec((1, D_MODEL), lambda i, f: (0, 0)),
    ]
    out_specs = [pl.BlockSpec((tm, D_MODEL), lambda i, f: (i, 0))]
    out_shape = [jax.ShapeDtypeStruct((s, D_MODEL), F32)]
    operands = [x, gain, w_up, w_up, conv_w, conv_w, conv_b, conv_b, w_down, final_gain]
    if cast_next:
        def tiling(w, rows_over_f):
            rows, cols = w.shape[1:]
            nr, nc = (nf, ni) if rows_over_f else (ni, nf)
            assert rows % nr == 0 and cols % nc == 0, (w.shape, ni, nf)
            if rows_over_f:
                return (rows // nr, cols // nc), lambda i, f: (f, i)
            return (rows // nr, cols // nc), lambda i, f: (i, f)

        w_in, w_up_next, w_down_next, w_out = next_weights
        for w, rows_over_f in ((w_in, False), (w_up_next, False), (w_down_next, True)):
            block, index = tiling(w, rows_over_f)
            in_specs.append(pl.BlockSpec((None,) + block,
                                         lambda i, f, index=index: (layer + 1,) + index(i, f)))
            out_specs.append(pl.BlockSpec(block, index))
            out_shape.append(jax.ShapeDtypeStruct(w.shape[1:], BF16))
        block = (w_out.shape[1] // ni, w_out.shape[2])
        in_specs.append(pl.BlockSpec((None,) + block, lambda i, f: (layer + 1, i, 0)))
        out_specs.append(pl.BlockSpec(block, lambda i, f: (i, 0)))
        out_shape.append(jax.ShapeDtypeStruct(w_out.shape[1:], BF16))
        operands += [w_in, w_up_next, w_down_next, w_out]
    return pl.pallas_call(
        kernel,
        grid=(ni, nf),
        in_specs=in_specs,
        out_specs=out_specs,
        out_shape=out_shape,
        scratch_shapes=[
            pltpu.VMEM((tm, D_MODEL), BF16),
            pltpu.VMEM((nf, SUBLANES, tf), F32),
            pltpu.VMEM((nf, SUBLANES, tf), F32),
        ] + [pltpu.VMEM((tm + SUBLANES, FFN_GROUP), F32)] * (2 * tf // FFN_GROUP),
        compiler_params=_params(("arbitrary", "arbitrary")),
        name="ffn",
    )(*operands)


def _rope_tables(seq):
    def tables(width, reps):
        rot = width // ROPE_FRACTION
        inv = 1.0 / (ROPE_THETA ** (jnp.arange(0, rot, 2, dtype=F32) / rot))
        ang = jnp.arange(seq, dtype=F32)[:, None] * inv[None, :]
        cos, sin = jnp.cos(ang), jnp.sin(ang)
        keep = jnp.concatenate([cos, cos, jnp.ones((seq, width - rot), F32)], axis=1)
        signed = jnp.concatenate([-sin, sin, jnp.zeros((seq, width - rot), F32)], axis=1)
        return jnp.stack([jnp.tile(t, (1, reps)) for t in (keep, signed)])

    return jnp.stack([tables(HEAD_DIM, 1), tables(DIFF_QK_DIM, 2)])


def kernel(x, norm_mix, w_in, lambda_q1, lambda_k1, lambda_q2, lambda_k2, diff_subln,
           sgu_ln_g, sgu_ln_b, sgu_w, sgu_b, w_out, norm_ffn, w_up, conv_w, conv_b, w_down,
           norm_final):
    batch, seq, _ = x.shape
    depth = w_in.shape[0]
    tables = _rope_tables(seq)
    rows = lambda v: v[:, None, :]
    norm_mix, norm_ffn, diff_subln, sgu_ln_g, sgu_ln_b, conv_b = map(
        rows, (norm_mix, norm_ffn, diff_subln, sgu_ln_g, sgu_ln_b, conv_b))
    lam_vecs = jnp.stack([lambda_q1, lambda_k1, lambda_q2, lambda_k2], axis=1)
    sgu_b_t = jnp.transpose(sgu_b, (0, 2, 1))
    outs = []
    for bi in range(batch):
        xs = x[bi]
        wi = w_in[0].astype(BF16)
        for l in range(depth):
            last = l == depth - 1
            lambda_init = 0.8 - 0.6 * math.exp(-0.3 * l)
            out_c, h = _spatial_gating(xs, norm_mix, wi, sgu_ln_g, sgu_ln_b, sgu_w, sgu_b_t, l,
                                       tm=1024)
            slabs_a, k_slabs_b, qv_t_b = _in_proj(h, wi, tables, tm=1024, t=DIFF_T)
            out_a = _dilated_mixture(slabs_a)
            out_b, *cast = _diff_attention(k_slabs_b, qv_t_b, lam_vecs, diff_subln, l,
                                           (w_up, w_down, w_out) if l == 0 else (),
                                           t=DIFF_T, lambda_init=lambda_init)
            if l == 0:
                wu, wd, wo = cast
            xs = _out_proj(xs, out_a, out_b, out_c, wo, tm=1024)
            xs, *cast = _ffn(xs, norm_ffn, wu, conv_w, conv_b, wd, l, norm_final.reshape(1, -1),
                             None if last else (w_in, w_up, w_down, w_out),
                             tm=512, tf=512, final_norm=last)
            if not last:
                wi, wu, wd, wo = cast
        outs.append(xs)
    return jnp.stack(outs)
```
